```python
import math
import jax, jax.numpy as jnp
from jax import lax
import numpy as np

D_MODEL = 1024
BATCH = 8
SEQ = 4096
DEPTH = 1

D_MIX = D_MODEL
D_CONV = D_MIX // 2
D_ATT = D_MIX - D_CONV
CONV_GROUP_DIM = 64
CONV_GROUPS = D_CONV // CONV_GROUP_DIM
HEAD_DIM = 64
ATT_HEADS = D_ATT // (2 * HEAD_DIM)
D_MIX_IN = 3 * D_CONV + 3 * D_ATT
D_FF = 2816
CONV_WIDTH = 3
Q_BLOCK = 128
NORM_EPS = 1e-6
SUBLN_EPS = 1e-5

kernel_name = "hybrid_shortconv_diffattn_convffn_block"


def rms_norm(x, g, eps=NORM_EPS):
    xf = x.astype(jnp.float32)
    y = xf * lax.rsqrt(jnp.mean(xf * xf, axis=-1, keepdims=True) + eps)
    return (y * g.astype(jnp.float32)).astype(x.dtype)


def dwconv3(x, w, b=None):
    xp = jnp.pad(x, ((0, 0), (1, 1), (0, 0)))
    y = xp[:, :-2] * w[0] + xp[:, 1:-1] * w[1] + xp[:, 2:] * w[2]
    if b is not None:
        y = y + b
    return y


def alibi_slopes(n_heads):
    return 2.0 ** (-8.0 * (jnp.arange(n_heads, dtype=jnp.float32) + 1.0) / n_heads)


def short_conv_mixer(u, conv_w, g_out):
    bsz, seq, _ = u.shape
    b_gate, c_gate, h_in = jnp.split(u, 3, axis=-1)
    y = b_gate * dwconv3(c_gate * h_in, conv_w)
    y = rms_norm(y.reshape(bsz, seq, CONV_GROUPS, CONV_GROUP_DIM),
                 g_out.reshape(CONV_GROUPS, CONV_GROUP_DIM))
    return y.reshape(bsz, seq, D_CONV)


def diff_attention(u, lam, g_subln, lambda_init):
    bsz, seq, _ = u.shape
    q, k, v = jnp.split(u, 3, axis=-1)
    q = q.reshape(bsz, seq, ATT_HEADS, 2, HEAD_DIM)
    k = k.reshape(bsz, seq, ATT_HEADS, 2, HEAD_DIM)
    v = v.reshape(bsz, seq, ATT_HEADS, 2 * HEAD_DIM)
    n_blk = seq // Q_BLOCK
    q_blocks = q.reshape(bsz, n_blk, Q_BLOCK, ATT_HEADS, 2, HEAD_DIM).transpose(1, 0, 2, 3, 4, 5)
    slopes = alibi_slopes(ATT_HEADS)
    k_pos = jnp.arange(seq, dtype=jnp.int32)
    scale = HEAD_DIM ** -0.5

    def one_block(args):
        q_blk, blk_idx = args
        q_pos = blk_idx * Q_BLOCK + jnp.arange(Q_BLOCK, dtype=jnp.int32)
        dist = jnp.abs(q_pos[:, None] - k_pos[None, :]).astype(jnp.float32)
        bias = -slopes[:, None, None] * dist
        s = jnp.einsum('bqhcd,bkhcd->bhcqk', q_blk, k).astype(jnp.float32) * scale
        p = jax.nn.softmax(s + bias[None, :, None], axis=-1)
        p_diff = (p[:, :, 0] - lam * p[:, :, 1]).astype(v.dtype)
        return jnp.einsum('bhqk,bkhe->bqhe', p_diff, v)

    o = lax.map(one_block, (q_blocks, jnp.arange(n_blk, dtype=jnp.int32)))
    o = o.transpose(1, 0, 2, 3, 4).reshape(bsz, seq, ATT_HEADS, 2 * HEAD_DIM)
    o = rms_norm(o, g_subln, eps=SUBLN_EPS) * (1.0 - lambda_init)
    return o.reshape(bsz, seq, D_ATT)


def conv_glu_ffn(h, w_up, conv_w, conv_b, w_down):
    gu = dwconv3(h @ w_up, conv_w, conv_b)
    gate, up = jnp.split(gu, 2, axis=-1)
    return (jax.nn.silu(gate) * up) @ w_down


def setup_inputs(seed: int = 0) -> dict:
    key = jax.random.key(seed)
    ks = jax.random.split(key, 20)
    f32 = jnp.float32

    def nrm(k, shape, scale):
        return jax.random.normal(k, shape, f32) * scale

    def gain(k, shape):
        return 1.0 + 0.05 * jax.random.normal(k, shape, f32)

    L = DEPTH
    return {
        "x": jax.random.normal(ks[0], (BATCH, SEQ, D_MODEL), f32),
        "g_mix_pre": gain(ks[1], (L, D_MODEL)),
        "w_mix_in": nrm(ks[2], (L, D_MODEL, D_MIX_IN), D_MODEL ** -0.5),
        "conv_w": nrm(ks[3], (L, CONV_WIDTH, D_CONV), CONV_WIDTH ** -0.5),
        "g_conv_out": gain(ks[4], (L, D_CONV)),
        "lambda_q1": nrm(ks[5], (L, HEAD_DIM), 0.1),
        "lambda_k1": nrm(ks[6], (L, HEAD_DIM), 0.1),
        "lambda_q2": nrm(ks[7], (L, HEAD_DIM), 0.1),
        "lambda_k2": nrm(ks[8], (L, HEAD_DIM), 0.1),
        "g_subln": gain(ks[9], (L, 2 * HEAD_DIM)),
        "w_mix_out": nrm(ks[10], (L, D_MIX, D_MODEL), D_MIX ** -0.5),
        "g_mix_post": gain(ks[11], (L, D_MODEL)),
        "g_ffn_pre": gain(ks[12], (L, D_MODEL)),
        "w_ffn_up": nrm(ks[13], (L, D_MODEL, 2 * D_FF), D_MODEL ** -0.5),
        "ffn_conv_w": nrm(ks[14], (L, CONV_WIDTH, 2 * D_FF), CONV_WIDTH ** -0.5),
        "ffn_conv_b": nrm(ks[15], (L, 2 * D_FF), 0.02),
        "w_ffn_down": nrm(ks[16], (L, D_FF, D_MODEL), D_FF ** -0.5),
        "g_ffn_post": gain(ks[17], (L, D_MODEL)),
    }


def reference(x, g_mix_pre, w_mix_in, conv_w, g_conv_out, lambda_q1, lambda_k1,
              lambda_q2, lambda_k2, g_subln, w_mix_out, g_mix_post, g_ffn_pre,
              w_ffn_up, ffn_conv_w, ffn_conv_b, w_ffn_down, g_ffn_post):
    for layer in range(DEPTH):
        lambda_init = 0.8 - 0.6 * math.exp(-0.3 * layer)
        lam = (jnp.exp(jnp.sum(lambda_q1[layer].astype(jnp.float32) * lambda_k1[layer].astype(jnp.float32)))
               - jnp.exp(jnp.sum(lambda_q2[layer].astype(jnp.float32) * lambda_k2[layer].astype(jnp.float32)))
               + lambda_init)

        h = rms_norm(x, g_mix_pre[layer])
        u = h @ w_mix_in[layer]
        y_conv = short_conv_mixer(u[..., :3 * D_CONV], conv_w[layer], g_conv_out[layer])
        y_att = diff_attention(u[..., 3 * D_CONV:], lam, g_subln[layer], lambda_init)
        mix = jnp.concatenate([y_conv, y_att], axis=-1) @ w_mix_out[layer]
        x = x + rms_norm(mix, g_mix_post[layer])

        h = rms_norm(x, g_ffn_pre[layer])
        f = conv_glu_ffn(h, w_ffn_up[layer], ffn_conv_w[layer], ffn_conv_b[layer], w_ffn_down[layer])
        x = x + rms_norm(f, g_ffn_post[layer])
    return x
```

```python
import functools
import math

import jax
import jax.numpy as jnp
from jax import lax
from jax.experimental import pallas as pl
from jax.experimental.pallas import tpu as pltpu

F32 = jnp.float32
BF16 = jnp.bfloat16

D_MODEL = 1024
D_CONV = 512
D_ATT = 512
CONV_GROUP_DIM = 64
HEAD_DIM = 64
V_DIM = 2 * HEAD_DIM
ATT_HEADS = D_ATT // V_DIM
D_MIX_IN = 3 * D_CONV + 3 * D_ATT
D_FF = 2816
NORM_EPS = 1e-6
SUBLN_EPS = 1e-5
QK_SCALE = HEAD_DIM ** -0.5

ROW_TILE = 512
Q_TILE = 512
K_TILE = 512
FFN_HALO = 16
CONV_HALO = 8
FFN_CHUNK = 256
VMEM_LIMIT = 56 * 1024 * 1024


def _rms_norm(x, g, eps):
    return x * lax.rsqrt(jnp.mean(x * x, axis=-1, keepdims=True) + eps) * g


def _in_proj_kernel(x_ref, g_ref, w_ref, b_ref, ch_ref, ka_ref, kb_ref, qt_ref, vt_ref):
    h = _rms_norm(x_ref[...], g_ref[...], NORM_EPS).astype(BF16)

    def proj(col):
        return jnp.dot(h, w_ref[:, col * 512:(col + 1) * 512], preferred_element_type=F32)

    b_ref[...] = proj(0)
    ch_ref[...] = proj(1) * proj(2)
    qt_ref[...] = (proj(3) * QK_SCALE).T.astype(BF16)
    k = proj(4)
    lane = lax.broadcasted_iota(jnp.int32, k.shape, 1)
    first_map = (lane & HEAD_DIM) == 0
    ka_ref[...] = jnp.where(first_map, k, 0.0).astype(BF16)
    kb_ref[...] = jnp.where(first_map, 0.0, k).astype(BF16)
    vt_ref[...] = proj(5).T.astype(BF16)


def _in_proj(x2, g, w):
    t = x2.shape[0]
    grid = (t // ROW_TILE,)
    row = lambda i: (i, 0)
    col = lambda i: (0, i)
    const = lambda i: (0, 0)
    return pl.pallas_call(
        _in_proj_kernel,
        grid=grid,
        in_specs=[
            pl.BlockSpec((ROW_TILE, D_MODEL), row),
            pl.BlockSpec((1, D_MODEL), const),
            pl.BlockSpec((D_MODEL, D_MIX_IN), const, pipeline_mode=pl.Buffered(1)),
        ],
        out_specs=[
            pl.BlockSpec((ROW_TILE, D_CONV), row),
            pl.BlockSpec((ROW_TILE, D_CONV), row),
            pl.BlockSpec((ROW_TILE, D_ATT), row),
            pl.BlockSpec((ROW_TILE, D_ATT), row),
            pl.BlockSpec((D_ATT, ROW_TILE), col),
            pl.BlockSpec((D_ATT, ROW_TILE), col),
        ],
        out_shape=[
            jax.ShapeDtypeStruct((t, D_CONV), F32),
            jax.ShapeDtypeStruct((t, D_CONV), F32),
            jax.ShapeDtypeStruct((t, D_ATT), BF16),
            jax.ShapeDtypeStruct((t, D_ATT), BF16),
            jax.ShapeDtypeStruct((D_ATT, t), BF16),
            jax.ShapeDtypeStruct((D_ATT, t), BF16),
        ],
        compiler_params=pltpu.CompilerParams(
            dimension_semantics=("arbitrary",), vmem_limit_bytes=VMEM_LIMIT),
        name="in_proj",
    )(x2, g, w)


def _attn_kernel(slopes_ref, lam_ref, qt_ref, ka_ref, kb_ref, vt_ref, g_ref, o_ref,
                 m_ref, l_ref, acc_ref, *, seq, lambda_init):
    head = pl.program_id(1)
    qi = pl.program_id(2)
    neg_slope = -slopes_ref[head]
    qt = qt_ref[...]
    d0 = (lax.broadcasted_iota(jnp.int32, (K_TILE, Q_TILE), 1)
          - lax.broadcasted_iota(jnp.int32, (K_TILE, Q_TILE), 0)).astype(F32)

    m_ref[...] = jnp.full(m_ref.shape, -1e30, F32)
    l_ref[...] = jnp.zeros(l_ref.shape, F32)
    acc_ref[...] = jnp.zeros(acc_ref.shape, F32)

    def body(j, carry):
        k0 = pl.multiple_of(j * K_TILE, K_TILE)
        off = (qi * Q_TILE - j * K_TILE).astype(F32)
        bias = neg_slope * jnp.abs(d0 + off)
        vt = vt_ref[:, pl.ds(k0, K_TILE)]
        for mp, k_ref in enumerate((ka_ref, kb_ref)):
            s = jnp.dot(k_ref[pl.ds(k0, K_TILE), :], qt, preferred_element_type=F32) + bias
            m_old = m_ref[mp]
            m_new = jnp.maximum(m_old, jnp.max(s, axis=0, keepdims=True))
            alpha = jnp.exp(m_old - m_new)
            p = jnp.exp(s - m_new)
            l_ref[mp] = alpha * l_ref[mp] + jnp.sum(p, axis=0, keepdims=True)
            acc_ref[mp] = alpha * acc_ref[mp] + jnp.dot(vt, p.astype(BF16), preferred_element_type=F32)
            m_ref[mp] = m_new
        return carry

    lax.fori_loop(0, seq // K_TILE, body, 0)

    lv = lam_ref[...]
    lam = (jnp.exp(jnp.sum(lv[0:1] * lv[1:2], axis=-1, keepdims=True))
           - jnp.exp(jnp.sum(lv[2:3] * lv[3:4], axis=-1, keepdims=True)) + lambda_init)
    o = acc_ref[0] / l_ref[0] - lam * (acc_ref[1] / l_ref[1])
    y = o * lax.rsqrt(jnp.mean(o * o, axis=0, keepdims=True) + SUBLN_EPS) * g_ref[...]
    o_ref[...] = (y * (1.0 - lambda_init)).T.astype(BF16)


def _attention(slopes, lam, qt, ka, kb, vt, g_col, *, batch, seq, lambda_init):
    t = batch * seq
    nq = seq // Q_TILE
    kv_rows = lambda b, h, i: (b, h)
    return pl.pallas_call(
        functools.partial(_attn_kernel, seq=seq, lambda_init=lambda_init),
        grid=(batch, ATT_HEADS, nq),
        in_specs=[
            pl.BlockSpec(memory_space=pltpu.SMEM),
            pl.BlockSpec((4, HEAD_DIM), lambda b, h, i: (0, 0)),
            pl.BlockSpec((V_DIM, Q_TILE), lambda b, h, i: (h, b * nq + i)),
            pl.BlockSpec((seq, V_DIM), kv_rows),
            pl.BlockSpec((seq, V_DIM), kv_rows),
            pl.BlockSpec((V_DIM, seq), lambda b, h, i: (h, b)),
            pl.BlockSpec((V_DIM, 1), lambda b, h, i: (0, 0)),
        ],
        out_specs=pl.BlockSpec((Q_TILE, V_DIM), lambda b, h, i: (b * nq + i, h)),
        out_shape=jax.ShapeDtypeStruct((t, D_ATT), BF16),
        scratch_shapes=[
            pltpu.VMEM((2, 1, Q_TILE), F32),
            pltpu.VMEM((2, 1, Q_TILE), F32),
            pltpu.VMEM((2, V_DIM, Q_TILE), F32),
        ],
        compiler_params=pltpu.CompilerParams(
            dimension_semantics=("arbitrary", "arbitrary", "arbitrary"), vmem_limit_bytes=VMEM_LIMIT),
        name="diff_attention",
    )(slopes, lam, qt, ka, kb, vt, g_col)


def _out_proj_kernel(b_ref, ch_ref, chp_ref, chn_ref, ya_ref, cw_ref, gc_ref, gmat_ref, w_ref,
                     x_ref, gp_ref, o_ref, *, tiles_per_seq):
    i = pl.program_id(0)
    has_prev = (i % tiles_per_seq != 0).astype(F32)
    has_next = (i % tiles_per_seq != tiles_per_seq - 1).astype(F32)
    ch = ch_ref[...]
    rows = lax.broadcasted_iota(jnp.int32, ch.shape, 0)
    prev = jnp.where(rows == 0, chp_ref[CONV_HALO - 1:CONV_HALO, :] * has_prev,
                     pltpu.roll(ch, 1, 0))
    nxt = jnp.where(rows == ROW_TILE - 1, chn_ref[0:1, :] * has_next,
                    pltpu.roll(ch, ROW_TILE - 1, 0))
    cw = cw_ref[...]
    y = b_ref[...] * (prev * cw[0:1] + ch * cw[1:2] + nxt * cw[2:3])
    z = y * y
    z_hi = z.astype(BF16)
    z_lo = (z - z_hi.astype(F32)).astype(BF16)
    gmat = gmat_ref[...]
    ms = (jnp.dot(z_hi, gmat, preferred_element_type=F32)
          + jnp.dot(z_lo, gmat, preferred_element_type=F32))
    yc = (y * lax.rsqrt(ms + NORM_EPS) * gc_ref[...]).astype(BF16)
    mix = (jnp.dot(yc, w_ref[0:D_CONV, :], preferred_element_type=F32)
           + jnp.dot(ya_ref[...], w_ref[D_CONV:, :], preferred_element_type=F32))
    o_ref[...] = x_ref[...] + _rms_norm(mix, gp_ref[...], NORM_EPS)


def _out_proj(b, ch, ya, cw, gc, gmat, w, x2, gp, *, seq):
    t = x2.shape[0]
    n = t // ROW_TILE
    hb = ROW_TILE // CONV_HALO
    n_halo = t // CONV_HALO
    row = lambda i: (i, 0)
    const = lambda i: (0, 0)
    return pl.pallas_call(
        functools.partial(_out_proj_kernel, tiles_per_seq=seq // ROW_TILE),
        grid=(n,),
        in_specs=[
            pl.BlockSpec((ROW_TILE, D_CONV), row),
            pl.BlockSpec((ROW_TILE, D_CONV), row),
            pl.BlockSpec((CONV_HALO, D_CONV), lambda i: (jnp.maximum(i * hb - 1, 0), 0)),
            pl.BlockSpec((CONV_HALO, D_CONV), lambda i: (jnp.minimum((i + 1) * hb, n_halo - 1), 0)),
            pl.BlockSpec((ROW_TILE, D_ATT), row),
            pl.BlockSpec((3, D_CONV), const),
            pl.BlockSpec((1, D_CONV), const),
            pl.BlockSpec((D_CONV, D_CONV), const),
            pl.BlockSpec((D_MODEL, D_MODEL), const, pipeline_mode=pl.Buffered(1)),
            pl.BlockSpec((ROW_TILE, D_MODEL), row),
            pl.BlockSpec((1, D_MODEL), const),
        ],
        out_specs=pl.BlockSpec((ROW_TILE, D_MODEL), row),
        out_shape=jax.ShapeDtypeStruct((t, D_MODEL), F32),
        compiler_params=pltpu.CompilerParams(
            dimension_semantics=("arbitrary",), vmem_limit_bytes=VMEM_LIMIT),
        name="out_proj",
    )(b, ch, ch, ch, ya, cw, gc, gmat, w, x2, gp)


def _ffn_kernel(x_ref, xp_ref, xn_ref, g_ref, wu_ref, cw_ref, cb_ref, wd_ref, gp_ref, o_ref,
                hs_ref, act_ref, *, tiles_per_seq):
    i = pl.program_id(0)
    has_prev = (i % tiles_per_seq != 0).astype(F32)
    has_next = (i % tiles_per_seq != tiles_per_seq - 1).astype(F32)
    g = g_ref[...]
    x = x_ref[...]
    hs_ref[0:FFN_HALO, :] = (_rms_norm(xp_ref[...], g, NORM_EPS) * has_prev).astype(BF16)
    hs_ref[FFN_HALO:FFN_HALO + ROW_TILE, :] = _rms_norm(x, g, NORM_EPS).astype(BF16)
    hs_ref[FFN_HALO + ROW_TILE:, :] = (_rms_norm(xn_ref[...], g, NORM_EPS) * has_next).astype(BF16)
    hs = hs_ref[...]
    ext = ROW_TILE + 2 * FFN_HALO

    def conv_chunk(c0):
        cols = slice(c0, c0 + FFN_CHUNK)
        u = jnp.dot(hs, wu_ref[:, cols], preferred_element_type=F32)
        cw = cw_ref[:, cols]
        mid = slice(FFN_HALO, FFN_HALO + ROW_TILE)
        return (pltpu.roll(u, 1, 0)[mid] * cw[0:1] + u[mid] * cw[1:2]
                + pltpu.roll(u, ext - 1, 0)[mid] * cw[2:3] + cb_ref[:, cols])

    for c in range(D_FF // FFN_CHUNK):
        gate = conv_chunk(c * FFN_CHUNK)
        up = conv_chunk(D_FF + c * FFN_CHUNK)
        act = gate * (1.0 / (1.0 + jnp.exp(-gate))) * up
        act_ref[:, c * FFN_CHUNK:(c + 1) * FFN_CHUNK] = act.astype(BF16)

    f = jnp.dot(act_ref[...], wd_ref[...], preferred_element_type=F32)
    o_ref[...] = x + _rms_norm(f, gp_ref[...], NORM_EPS)


def _ffn(x1, g, wu, cw, cb, wd, gp, *, seq):
    t = x1.shape[0]
    n = t // ROW_TILE
    hb = ROW_TILE // FFN_HALO
    n_halo = t // FFN_HALO
    row = lambda i: (i, 0)
    const = lambda i: (0, 0)
    return pl.pallas_call(
        functools.partial(_ffn_kernel, tiles_per_seq=seq // ROW_TILE),
        grid=(n,),
        in_specs=[
            pl.BlockSpec((ROW_TILE, D_MODEL), row),
            pl.BlockSpec((FFN_HALO, D_MODEL), lambda i: (jnp.maximum(i * hb - 1, 0), 0)),
            pl.BlockSpec((FFN_HALO, D_MODEL), lambda i: (jnp.minimum((i + 1) * hb, n_halo - 1), 0)),
            pl.BlockSpec((1, D_MODEL), const),
            pl.BlockSpec((D_MODEL, 2 * D_FF), const, pipeline_mode=pl.Buffered(1)),
            pl.BlockSpec((3, 2 * D_FF), const),
            pl.BlockSpec((1, 2 * D_FF), const),
            pl.BlockSpec((D_FF, D_MODEL), const, pipeline_mode=pl.Buffered(1)),
            pl.BlockSpec((1, D_MODEL), const),
        ],
        out_specs=pl.BlockSpec((ROW_TILE, D_MODEL), row),
        out_shape=jax.ShapeDtypeStruct((t, D_MODEL), F32),
        scratch_shapes=[
            pltpu.VMEM((ROW_TILE + 2 * FFN_HALO, D_MODEL), BF16),
            pltpu.VMEM((ROW_TILE, D_FF), BF16),
        ],
        compiler_params=pltpu.CompilerParams(
            dimension_semantics=("arbitrary",), vmem_limit_bytes=VMEM_LIMIT),
        name="conv_glu_ffn",
    )(x1, x1, x1, g, wu, cw, cb, wd, gp)


def kernel(x, g_mix_pre, w_mix_in, conv_w, g_conv_out, lambda_q1, lambda_k1, lambda_q2, lambda_k2,
           g_subln, w_mix_out, g_mix_post, g_ffn_pre, w_ffn_up, ffn_conv_w, ffn_conv_b, w_ffn_down,
           g_ffn_post):
    batch, seq, d = x.shape
    depth = g_mix_pre.shape[0]
    assert d == D_MODEL and seq % ROW_TILE == 0 and seq % Q_TILE == 0 and seq % K_TILE == 0
    t = batch * seq
    x2 = x.reshape(t, d)
    slopes = 2.0 ** (-8.0 * (jnp.arange(ATT_HEADS, dtype=F32) + 1.0) / ATT_HEADS)
    group = jnp.arange(D_CONV) // CONV_GROUP_DIM
    gmat = jnp.where(group[:, None] == group[None, :], 1.0 / CONV_GROUP_DIM, 0.0).astype(BF16)

    for layer in range(depth):
        lambda_init = 0.8 - 0.6 * math.exp(-0.3 * layer)
        lam = jnp.stack([lambda_q1[layer], lambda_k1[layer], lambda_q2[layer], lambda_k2[layer]]).astype(F32)
        b, ch, ka, kb, qt, vt = _in_proj(x2, g_mix_pre[layer][None], w_mix_in[layer].astype(BF16))
        ya = _attention(slopes, lam, qt, ka, kb, vt, g_subln[layer][:, None],
                        batch=batch, seq=seq, lambda_init=lambda_init)
        x2 = _out_proj(b, ch, ya, conv_w[layer], g_conv_out[layer][None], gmat,
                       w_mix_out[layer].astype(BF16), x2, g_mix_post[layer][None], seq=seq)
        x2 = _ffn(x2, g_ffn_pre[layer][None], w_ffn_up[layer].astype(BF16), ffn_conv_w[layer],
                  ffn_conv_b[layer][None], w_ffn_down[layer].astype(BF16), g_ffn_post[layer][None],
                  seq=seq)
    return x2.reshape(batch, seq, d)
```

```python
import functools
import math

import jax
import jax.numpy as jnp
from jax import lax
from jax.experimental import pallas as pl
from jax.experimental.pallas import tpu as pltpu

F32 = jnp.float32
BF16 = jnp.bfloat16

D_MODEL = 1024
D_CONV = 512
D_ATT = 512
CONV_GROUP_DIM = 64
HEAD_DIM = 64
V_DIM = 2 * HEAD_DIM
ATT_HEADS = D_ATT // V_DIM
D_MIX_IN = 3 * D_CONV + 3 * D_ATT
D_FF = 2816
NORM_EPS = 1e-6
SUBLN_EPS = 1e-5
QK_SCALE = HEAD_DIM ** -0.5

ROW_TILE = 512
Q_TILE = 512
K_TILE = 512
FFN_HALO = 16
CONV_HALO = 8
FFN_CHUNK = 256
POS_SPLIT = 64
VMEM_LIMIT = 56 * 1024 * 1024


def _rms_norm(x, g, eps):
    return x * lax.rsqrt(jnp.mean(x * x, axis=-1, keepdims=True) + eps) * g


def _in_proj_kernel(x_ref, g_ref, coef_ref, w_ref, b_ref, ch_ref, ka_ref, kb_ref, qa_ref, qb_ref,
                    vt_ref, *, tiles_per_seq):
    h = _rms_norm(x_ref[...], g_ref[...], NORM_EPS).astype(BF16)

    def proj(col):
        return jnp.dot(h, w_ref[:, col * D_CONV:(col + 1) * D_CONV], preferred_element_type=F32)

    b_ref[...] = proj(0)
    ch_ref[...] = proj(1) * proj(2)

    pos = ((pl.program_id(0) % tiles_per_seq) * ROW_TILE
           + lax.broadcasted_iota(jnp.int32, (ROW_TILE, 1), 0))
    hi = (pos // POS_SPLIT).astype(F32)
    lo = (pos % POS_SPLIT).astype(F32)
    coef = coef_ref[...]
    k_pos = hi * coef[0:1] + lo * coef[1:2] + coef[2:3]
    q_pos = hi * coef[3:4] + lo * coef[4:5] + coef[5:6]
    lane = lax.broadcasted_iota(jnp.int32, (ROW_TILE, D_ATT), 1)
    first_map = (lane & HEAD_DIM) == 0
    q = proj(3) * QK_SCALE
    qa_ref[...] = jnp.where(first_map, q, q_pos).T.astype(BF16)
    qb_ref[...] = jnp.where(first_map, q_pos, q).T.astype(BF16)
    k = proj(4)
    ka_ref[...] = jnp.where(first_map, k, k_pos).astype(BF16)
    kb_ref[...] = jnp.where(first_map, k_pos, k).astype(BF16)
    vt_ref[...] = proj(5).T.astype(BF16)


def _position_coefficients():
    lane = jnp.arange(D_ATT)
    e = lane % HEAD_DIM
    slope = 2.0 ** (-8.0 * ((lane // V_DIM).astype(F32) + 1.0) / ATT_HEADS)
    zero = jnp.zeros((D_ATT,), F32)
    rows = [
        jnp.where(e == 0, slope * POS_SPLIT, 0.0), jnp.where(e == 1, slope, 0.0),
        jnp.where((e == 2) | (e == 3), 1.0, 0.0),
        jnp.where(e == 2, -slope * POS_SPLIT, 0.0), jnp.where(e == 3, -slope, 0.0),
        jnp.where((e == 0) | (e == 1), 1.0, 0.0),
        zero, zero,
    ]
    return jnp.stack(rows).astype(F32)


def _in_proj(x2, g, coef, w, *, seq):
    t = x2.shape[0]
    grid = (t // ROW_TILE,)
    row = lambda i: (i, 0)
    col = lambda i: (0, i)
    const = lambda i: (0, 0)
    return pl.pallas_call(
        functools.partial(_in_proj_kernel, tiles_per_seq=seq // ROW_TILE),
        grid=grid,
        in_specs=[
            pl.BlockSpec((ROW_TILE, D_MODEL), row),
            pl.BlockSpec((1, D_MODEL), const),
            pl.BlockSpec((8, D_ATT), const),
            pl.BlockSpec((D_MODEL, D_MIX_IN), const, pipeline_mode=pl.Buffered(1)),
        ],
        out_specs=[
            pl.BlockSpec((ROW_TILE, D_CONV), row),
            pl.BlockSpec((ROW_TILE, D_CONV), row),
            pl.BlockSpec((ROW_TILE, D_ATT), row),
            pl.BlockSpec((ROW_TILE, D_ATT), row),
            pl.BlockSpec((D_ATT, ROW_TILE), col),
            pl.BlockSpec((D_ATT, ROW_TILE), col),
            pl.BlockSpec((D_ATT, ROW_TILE), col),
        ],
        out_shape=[
            jax.ShapeDtypeStruct((t, D_CONV), F32),
            jax.ShapeDtypeStruct((t, D_CONV), F32),
            jax.ShapeDtypeStruct((t, D_ATT), BF16),
            jax.ShapeDtypeStruct((t, D_ATT), BF16),
            jax.ShapeDtypeStruct((D_ATT, t), BF16),
            jax.ShapeDtypeStruct((D_ATT, t), BF16),
            jax.ShapeDtypeStruct((D_ATT, t), BF16),
        ],
        compiler_params=pltpu.CompilerParams(
            dimension_semantics=("arbitrary",), vmem_limit_bytes=VMEM_LIMIT),
        name="in_proj",
    )(x2, g, coef, w)


def _attn_kernel(slopes_ref, lam_ref, qa_ref, qb_ref, ka_ref, kb_ref, vt_ref, g_ref, o_ref,
                 m_ref, l_ref, acc_ref, s0_ref, s1_ref, *, seq, lambda_init):
    head = pl.program_id(1)
    qi = pl.program_id(2)
    nk = seq // K_TILE

    row = lax.broadcasted_iota(jnp.int32, (V_DIM, Q_TILE), 0)
    qa_before = qa_ref[...]
    qb_before = qb_ref[...]
    qa_after = jnp.where(row >= HEAD_DIM, -qa_before, qa_before)
    qb_after = jnp.where(row < HEAD_DIM, -qb_before, qb_before)

    m_ref[...] = jnp.full(m_ref.shape, -1e30, F32)
    l_ref[...] = jnp.zeros(l_ref.shape, F32)
    acc_ref[...] = jnp.zeros(acc_ref.shape, F32)

    def tile_of(n):
        return jnp.where(n == 0, qi, jnp.where(n <= qi, n - 1, n))

    def scores(n, s_ref, corr=None):
        j = tile_of(n)
        k0 = pl.multiple_of(j * K_TILE, K_TILE)
        keys_before = j <= qi
        for mp, (k_ref, q_before, q_after) in enumerate(
                ((ka_ref, qa_before, qa_after), (kb_ref, qb_before, qb_after))):
            qt = jnp.where(keys_before, q_before, q_after)
            s = jnp.dot(k_ref[pl.ds(k0, K_TILE), :], qt, preferred_element_type=F32)
            s_ref[mp] = s if corr is None else s + corr

    def absorb(n, s_ref):
        k0 = pl.multiple_of(tile_of(n) * K_TILE, K_TILE)
        vt = vt_ref[:, pl.ds(k0, K_TILE)]
        for mp in range(2):
            s = s_ref[mp]
            m_old = m_ref[mp]
            m_new = jnp.maximum(m_old, jnp.max(s, axis=0, keepdims=True))
            alpha = jnp.exp(m_old - m_new)
            p = jnp.exp(s - m_new)
            l_ref[mp] = alpha * l_ref[mp] + jnp.sum(p, axis=0, keepdims=True)
            acc_ref[mp] = alpha * acc_ref[mp] + jnp.dot(vt, p.astype(BF16), preferred_element_type=F32)
            m_ref[mp] = m_new

    key_minus_query = (lax.broadcasted_iota(jnp.int32, (K_TILE, Q_TILE), 0)
                       - lax.broadcasted_iota(jnp.int32, (K_TILE, Q_TILE), 1))
    corr = (-2.0 * slopes_ref[head]) * jnp.maximum(key_minus_query, 0).astype(F32)
    scores(0, s0_ref, corr)

    def pair(p, carry):
        n = 2 * p
        scores(n + 1, s1_ref)
        absorb(n, s0_ref)
        scores(n + 2, s0_ref)
        absorb(n + 1, s1_ref)
        return carry

    lax.fori_loop(0, (nk - 2) // 2, pair, 0)
    scores(nk - 1, s1_ref)
    absorb(nk - 2, s0_ref)
    absorb(nk - 1, s1_ref)

    lv = lam_ref[...]
    lam = (jnp.exp(jnp.sum(lv[0:1] * lv[1:2], axis=-1, keepdims=True))
           - jnp.exp(jnp.sum(lv[2:3] * lv[3:4], axis=-1, keepdims=True)) + lambda_init)
    o = acc_ref[0] / l_ref[0] - lam * (acc_ref[1] / l_ref[1])
    y = o * lax.rsqrt(jnp.mean(o * o, axis=0, keepdims=True) + SUBLN_EPS) * g_ref[...]
    o_ref[...] = (y * (1.0 - lambda_init)).T.astype(BF16)


def _attention(slopes, lam, qa, qb, ka, kb, vt, g_col, *, batch, seq, lambda_init):
    assert Q_TILE == K_TILE and (seq // K_TILE) % 2 == 0
    assert 8 % ATT_HEADS == 0
    t = batch * seq
    nq = seq // Q_TILE
    kv_rows = lambda b, h, i: (b, h)
    q_cols = lambda b, h, i: (h, b * nq + i)
    return pl.pallas_call(
        functools.partial(_attn_kernel, seq=seq, lambda_init=lambda_init),
        grid=(batch, ATT_HEADS, nq),
        in_specs=[
            pl.BlockSpec(memory_space=pltpu.SMEM),
            pl.BlockSpec((4, HEAD_DIM), lambda b, h, i: (0, 0)),
            pl.BlockSpec((V_DIM, Q_TILE), q_cols),
            pl.BlockSpec((V_DIM, Q_TILE), q_cols),
            pl.BlockSpec((seq, V_DIM), kv_rows),
            pl.BlockSpec((seq, V_DIM), kv_rows),
            pl.BlockSpec((V_DIM, seq), lambda b, h, i: (h, b)),
            pl.BlockSpec((V_DIM, 1), lambda b, h, i: (0, 0)),
        ],
        out_specs=pl.BlockSpec((Q_TILE, V_DIM), lambda b, h, i: (b * nq + i, h)),
        out_shape=jax.ShapeDtypeStruct((t, D_ATT), BF16),
        scratch_shapes=[
            pltpu.VMEM((2, 1, Q_TILE), F32),
            pltpu.VMEM((2, 1, Q_TILE), F32),
            pltpu.VMEM((2, V_DIM, Q_TILE), F32),
            pltpu.VMEM((2, K_TILE, Q_TILE), F32),
            pltpu.VMEM((2, K_TILE, Q_TILE), F32),
        ],
        compiler_params=pltpu.CompilerParams(
            dimension_semantics=("arbitrary", "arbitrary", "arbitrary"), vmem_limit_bytes=VMEM_LIMIT),
        name="diff_attention",
    )(slopes, lam, qa, qb, ka, kb, vt, g_col)


def _out_proj_kernel(b_ref, ch_ref, chp_ref, chn_ref, ya_ref, cw_ref, gc_ref, gmat_ref, w_ref,
                     x_ref, gp_ref, o_ref, *, tiles_per_seq):
    i = pl.program_id(0)
    has_prev = (i % tiles_per_seq != 0).astype(F32)
    has_next = (i % tiles_per_seq != tiles_per_seq - 1).astype(F32)
    ch = ch_ref[...]
    rows = lax.broadcasted_iota(jnp.int32, ch.shape, 0)
    prev = jnp.where(rows == 0, chp_ref[CONV_HALO - 1:CONV_HALO, :] * has_prev,
                     pltpu.roll(ch, 1, 0))
    nxt = jnp.where(rows == ROW_TILE - 1, chn_ref[0:1, :] * has_next,
                    pltpu.roll(ch, ROW_TILE - 1, 0))
    cw = cw_ref[...]
    y = b_ref[...] * (prev * cw[0:1] + ch * cw[1:2] + nxt * cw[2:3])
    z = y * y
    z_hi = z.astype(BF16)
    z_lo = (z - z_hi.astype(F32)).astype(BF16)
    gmat = gmat_ref[...]
    ms = (jnp.dot(z_hi, gmat, preferred_element_type=F32)
          + jnp.dot(z_lo, gmat, preferred_element_type=F32))
    yc = (y * lax.rsqrt(ms + NORM_EPS) * gc_ref[...]).astype(BF16)
    mix = (jnp.dot(yc, w_ref[0:D_CONV, :], preferred_element_type=F32)
           + jnp.dot(ya_ref[...], w_ref[D_CONV:, :], preferred_element_type=F32))
    o_ref[...] = x_ref[...] + _rms_norm(mix, gp_ref[...], NORM_EPS)


def _out_proj(b, ch, ya, cw, gc, gmat, w, x2, gp, *, seq):
    t = x2.shape[0]
    n = t // ROW_TILE
    hb = ROW_TILE // CONV_HALO
    n_halo = t // CONV_HALO
    row = lambda i: (i, 0)
    const = lambda i: (0, 0)
    return pl.pallas_call(
        functools.partial(_out_proj_kernel, tiles_per_seq=seq // ROW_TILE),
        grid=(n,),
        in_specs=[
            pl.BlockSpec((ROW_TILE, D_CONV), row),
            pl.BlockSpec((ROW_TILE, D_CONV), row),
            pl.BlockSpec((CONV_HALO, D_CONV), lambda i: (jnp.maximum(i * hb - 1, 0), 0)),
            pl.BlockSpec((CONV_HALO, D_CONV), lambda i: (jnp.minimum((i + 1) * hb, n_halo - 1), 0)),
            pl.BlockSpec((ROW_TILE, D_ATT), row),
            pl.BlockSpec((3, D_CONV), const),
            pl.BlockSpec((1, D_CONV), const),
            pl.BlockSpec((D_CONV, D_CONV), const),
            pl.BlockSpec((D_MODEL, D_MODEL), const, pipeline_mode=pl.Buffered(1)),
            pl.BlockSpec((ROW_TILE, D_MODEL), row),
            pl.BlockSpec((1, D_MODEL), const),
        ],
        out_specs=pl.BlockSpec((ROW_TILE, D_MODEL), row),
        out_shape=jax.ShapeDtypeStruct((t, D_MODEL), F32),
        compiler_params=pltpu.CompilerParams(
            dimension_semantics=("arbitrary",), vmem_limit_bytes=VMEM_LIMIT),
        name="out_proj",
    )(b, ch, ch, ch, ya, cw, gc, gmat, w, x2, gp)


def _ffn_kernel(x_ref, xp_ref, xn_ref, g_ref, wu_ref, cw_ref, cb_ref, wd_ref, gp_ref, o_ref,
                hs_ref, act_ref, *, tiles_per_seq):
    i = pl.program_id(0)
    has_prev = (i % tiles_per_seq != 0).astype(F32)
    has_next = (i % tiles_per_seq != tiles_per_seq - 1).astype(F32)
    g = g_ref[...]
    x = x_ref[...]
    hs_ref[0:FFN_HALO, :] = (_rms_norm(xp_ref[...], g, NORM_EPS) * has_prev).astype(BF16)
    hs_ref[FFN_HALO:FFN_HALO + ROW_TILE, :] = _rms_norm(x, g, NORM_EPS).astype(BF16)
    hs_ref[FFN_HALO + ROW_TILE:, :] = (_rms_norm(xn_ref[...], g, NORM_EPS) * has_next).astype(BF16)
    hs = hs_ref[...]
    ext = ROW_TILE + 2 * FFN_HALO

    def conv_chunk(c0):
        cols = slice(c0, c0 + FFN_CHUNK)
        u = jnp.dot(hs, wu_ref[:, cols], preferred_element_type=F32)
        cw = cw_ref[:, cols]
        mid = slice(FFN_HALO, FFN_HALO + ROW_TILE)
        return (pltpu.roll(u, 1, 0)[mid] * cw[0:1] + u[mid] * cw[1:2]
                + pltpu.roll(u, ext - 1, 0)[mid] * cw[2:3] + cb_ref[:, cols])

    for c in range(D_FF // FFN_CHUNK):
        gate = conv_chunk(c * FFN_CHUNK)
        up = conv_chunk(D_FF + c * FFN_CHUNK)
        act = gate * (1.0 / (1.0 + jnp.exp(-gate))) * up
        act_ref[:, c * FFN_CHUNK:(c + 1) * FFN_CHUNK] = act.astype(BF16)

    f = jnp.dot(act_ref[...], wd_ref[...], preferred_element_type=F32)
    o_ref[...] = x + _rms_norm(f, gp_ref[...], NORM_EPS)


def _ffn(x1, g, wu, cw, cb, wd, gp, *, seq):
    t = x1.shape[0]
    n = t // ROW_TILE
    hb = ROW_TILE // FFN_HALO
    n_halo = t // FFN_HALO
    row = lambda i: (i, 0)
    const = lambda i: (0, 0)
    return pl.pallas_call(
        functools.partial(_ffn_kernel, tiles_per_seq=seq // ROW_TILE),
        grid=(n,),
        in_specs=[
            pl.BlockSpec((ROW_TILE, D_MODEL), row),
            pl.BlockSpec((FFN_HALO, D_MODEL), lambda i: (jnp.maximum(i * hb - 1, 0), 0)),
            pl.BlockSpec((FFN_HALO, D_MODEL), lambda i: (jnp.minimum((i + 1) * hb, n_halo - 1), 0)),
            pl.BlockSpec((1, D_MODEL), const),
            pl.BlockSpec((D_MODEL, 2 * D_FF), const, pipeline_mode=pl.Buffered(1)),
            pl.BlockSpec((3, 2 * D_FF), const),
            pl.BlockSpec((1, 2 * D_FF), const),
            pl.BlockSpec((D_FF, D_MODEL), const, pipeline_mode=pl.Buffered(1)),
            pl.BlockSpec((1, D_MODEL), const),
        ],
        out_specs=pl.BlockSpec((ROW_TILE, D_MODEL), row),
        out_shape=jax.ShapeDtypeStruct((t, D_MODEL), F32),
        scratch_shapes=[
            pltpu.VMEM((ROW_TILE + 2 * FFN_HALO, D_MODEL), BF16),
            pltpu.VMEM((ROW_TILE, D_FF), BF16),
        ],
        compiler_params=pltpu.CompilerParams(
            dimension_semantics=("arbitrary",), vmem_limit_bytes=VMEM_LIMIT),
        name="conv_glu_ffn",
    )(x1, x1, x1, g, wu, cw, cb, wd, gp)


def kernel(x, g_mix_pre, w_mix_in, conv_w, g_conv_out, lambda_q1, lambda_k1, lambda_q2, lambda_k2,
           g_subln, w_mix_out, g_mix_post, g_ffn_pre, w_ffn_up, ffn_conv_w, ffn_conv_b, w_ffn_down,
           g_ffn_post):
    batch, seq, d = x.shape
    depth = g_mix_pre.shape[0]
    assert d == D_MODEL and seq % ROW_TILE == 0 and seq % Q_TILE == 0 and seq % K_TILE == 0
    assert seq <= POS_SPLIT * POS_SPLIT
    t = batch * seq
    x2 = x.reshape(t, d)
    slopes = 2.0 ** (-8.0 * (jnp.arange(ATT_HEADS, dtype=F32) + 1.0) / ATT_HEADS)
    coef = _position_coefficients()
    group = jnp.arange(D_CONV) // CONV_GROUP_DIM
    gmat = jnp.where(group[:, None] == group[None, :], 1.0 / CONV_GROUP_DIM, 0.0).astype(BF16)

    for layer in range(depth):
        lambda_init = 0.8 - 0.6 * math.exp(-0.3 * layer)
        lam = jnp.stack([lambda_q1[layer], lambda_k1[layer], lambda_q2[layer], lambda_k2[layer]]).astype(F32)
        b, ch, ka, kb, qa, qb, vt = _in_proj(x2, g_mix_pre[layer][None], coef,
                                             w_mix_in[layer].astype(BF16), seq=seq)
        ya = _attention(slopes, lam, qa, qb, ka, kb, vt, g_subln[layer][:, None],
                        batch=batch, seq=seq, lambda_init=lambda_init)
        x2 = _out_proj(b, ch, ya, conv_w[layer], g_conv_out[layer][None], gmat,
                       w_mix_out[layer].astype(BF16), x2, g_mix_post[layer][None], seq=seq)
        x2 = _ffn(x2, g_ffn_pre[layer][None], w_ffn_up[layer].astype(BF16), ffn_conv_w[layer],
                  ffn_conv_b[layer][None], w_ffn_down[layer].astype(BF16), g_ffn_post[layer][None],
                  seq=seq)
    return x2.reshape(batch, seq, d)
```

```python
import functools
import math

import jax
import jax.numpy as jnp
import numpy as np
from jax import lax
from jax.experimental import pallas as pl
from jax.experimental.pallas import tpu as pltpu

F32 = jnp.float32
BF16 = jnp.bfloat16

D_MODEL = 1024
D_CONV = 512
D_ATT = 512
CONV_GROUP_DIM = 64
HEAD_DIM = 64
V_DIM = 2 * HEAD_DIM
ATT_HEADS = D_ATT // V_DIM
D_MIX_IN = 3 * D_CONV + 3 * D_ATT
D_FF = 2816
NORM_EPS = 1e-6
SUBLN_EPS = 1e-5
QK_SCALE = HEAD_DIM ** -0.5
LOG2_E = math.log2(math.e)

ROW_TILE = 512
Q_TILE = 512
K_TILE = 512
FFN_HALO = 16
CONV_HALO = 8
FFN_CHUNK = 256
POS_SPLIT = 64
VMEM_LIMIT = 56 * 1024 * 1024


def _rms_norm(x, g, eps):
    return x * lax.rsqrt(jnp.mean(x * x, axis=-1, keepdims=True) + eps) * g


def _in_proj_kernel(x_ref, g_ref, coef_ref, w_ref, b_ref, ch_ref, ka_ref, kb_ref, qa_ref, qb_ref,
                    vt_ref, *, tiles_per_seq):
    h = _rms_norm(x_ref[...], g_ref[...], NORM_EPS).astype(BF16)

    def proj(col):
        return jnp.dot(h, w_ref[:, col * D_CONV:(col + 1) * D_CONV], preferred_element_type=F32)

    b_ref[...] = proj(0)
    ch_ref[...] = proj(1) * proj(2)

    pos = ((pl.program_id(0) % tiles_per_seq) * ROW_TILE
           + lax.broadcasted_iota(jnp.int32, (ROW_TILE, 1), 0))
    hi = (pos // POS_SPLIT).astype(F32)
    lo = (pos % POS_SPLIT).astype(F32)
    coef = coef_ref[...]
    k_pos = hi * coef[0:1] + lo * coef[1:2] + coef[2:3]
    q_pos = hi * coef[3:4] + lo * coef[4:5] + coef[5:6]
    lane = lax.broadcasted_iota(jnp.int32, (ROW_TILE, D_ATT), 1)
    first_map = (lane & HEAD_DIM) == 0
    q = proj(3) * (QK_SCALE * LOG2_E)
    qa_ref[...] = jnp.where(first_map, q, q_pos).T.astype(BF16)
    qb_ref[...] = jnp.where(first_map, q_pos, q).T.astype(BF16)
    k = proj(4)
    ka_ref[...] = jnp.where(first_map, k, k_pos).astype(BF16)
    kb_ref[...] = jnp.where(first_map, k_pos, k).astype(BF16)
    vt_ref[...] = proj(5).T.astype(BF16)


def _position_coefficients():
    lane = jnp.arange(D_ATT)
    e = lane % HEAD_DIM
    slope = 2.0 ** (-8.0 * ((lane // V_DIM).astype(F32) + 1.0) / ATT_HEADS)
    pieces = []
    rest = np.float64(LOG2_E)
    for _ in range(3):
        piece = np.float64(rest.astype(BF16))
        pieces.append(piece)
        rest = rest - piece
    log2e_piece = jnp.asarray(pieces, F32)[e % 3]
    zero = jnp.zeros((D_ATT,), F32)
    rows = [
        jnp.where(e < 3, slope * POS_SPLIT, 0.0),
        jnp.where((e >= 3) & (e < 6), slope, 0.0),
        jnp.where((e >= 6) & (e < 12), log2e_piece, 0.0),
        jnp.where((e >= 6) & (e < 9), -slope * POS_SPLIT, 0.0),
        jnp.where((e >= 9) & (e < 12), -slope, 0.0),
        jnp.where(e < 6, log2e_piece, 0.0),
        zero, zero,
    ]
    return jnp.stack(rows).astype(F32)


def _in_proj(x2, g, coef, w, *, seq):
    t = x2.shape[0]
    grid = (t // ROW_TILE,)
    row = lambda i: (i, 0)
    col = lambda i: (0, i)
    const = lambda i: (0, 0)
    return pl.pallas_call(
        functools.partial(_in_proj_kernel, tiles_per_seq=seq // ROW_TILE),
        grid=grid,
        in_specs=[
            pl.BlockSpec((ROW_TILE, D_MODEL), row),
            pl.BlockSpec((1, D_MODEL), const),
            pl.BlockSpec((8, D_ATT), const),
            pl.BlockSpec((D_MODEL, D_MIX_IN), const, pipeline_mode=pl.Buffered(1)),
        ],
        out_specs=[
            pl.BlockSpec((ROW_TILE, D_CONV), row),
            pl.BlockSpec((ROW_TILE, D_CONV), row),
            pl.BlockSpec((ROW_TILE, D_ATT), row),
            pl.BlockSpec((ROW_TILE, D_ATT), row),
            pl.BlockSpec((D_ATT, ROW_TILE), col),
            pl.BlockSpec((D_ATT, ROW_TILE), col),
            pl.BlockSpec((D_ATT, ROW_TILE), col),
        ],
        out_shape=[
            jax.ShapeDtypeStruct((t, D_CONV), F32),
            jax.ShapeDtypeStruct((t, D_CONV), F32),
            jax.ShapeDtypeStruct((t, D_ATT), BF16),
            jax.ShapeDtypeStruct((t, D_ATT), BF16),
            jax.ShapeDtypeStruct((D_ATT, t), BF16),
            jax.ShapeDtypeStruct((D_ATT, t), BF16),
            jax.ShapeDtypeStruct((D_ATT, t), BF16),
        ],
        compiler_params=pltpu.CompilerParams(
            dimension_semantics=("arbitrary",), vmem_limit_bytes=VMEM_LIMIT),
        name="in_proj",
    )(x2, g, coef, w)


def _attn_kernel(slopes_ref, lam_ref, qa_ref, qb_ref, ka_ref, kb_ref, vt_ref, g_ref, o_ref,
                 m_ref, l_ref, acc_ref, s0_ref, s1_ref, smax0_ref, smax1_ref, *, seq, lambda_init):
    head = pl.program_id(1)
    qi = pl.program_id(2)
    nk = seq // K_TILE

    row = lax.broadcasted_iota(jnp.int32, (V_DIM, Q_TILE), 0)
    qa_before = qa_ref[...]
    qb_before = qb_ref[...]
    qa_after = jnp.where(row >= HEAD_DIM, -qa_before, qa_before)
    qb_after = jnp.where(row < HEAD_DIM, -qb_before, qb_before)

    m_ref[...] = jnp.full(m_ref.shape, -1e30, F32)
    l_ref[...] = jnp.zeros(l_ref.shape, F32)
    acc_ref[...] = jnp.zeros(acc_ref.shape, F32)

    def tile_of(n):
        return jnp.where(n == 0, qi, jnp.where(n <= qi, n - 1, n))

    buf0 = (s0_ref, smax0_ref)
    buf1 = (s1_ref, smax1_ref)

    def scores(n, buf, corr=None):
        s_ref, smax_ref = buf
        j = tile_of(n)
        k0 = pl.multiple_of(j * K_TILE, K_TILE)
        keys_before = j <= qi
        for mp, (k_ref, q_before, q_after) in enumerate(
                ((ka_ref, qa_before, qa_after), (kb_ref, qb_before, qb_after))):
            qt = jnp.where(keys_before, q_before, q_after)
            s = jnp.dot(k_ref[pl.ds(k0, K_TILE), :], qt, preferred_element_type=F32)
            if corr is not None:
                s = s + corr
            s_ref[mp] = s
            smax_ref[mp] = jnp.max(s, axis=0, keepdims=True)

    def absorb(n, buf):
        s_ref, smax_ref = buf
        k0 = pl.multiple_of(tile_of(n) * K_TILE, K_TILE)
        vt = vt_ref[:, pl.ds(k0, K_TILE)]
        for mp in range(2):
            s = s_ref[mp]
            m_old = m_ref[mp]
            m_new = jnp.maximum(m_old, smax_ref[mp])
            alpha = jnp.exp2(m_old - m_new)
            p = jnp.exp2(s - m_new)
            l_ref[mp] = alpha * l_ref[mp] + jnp.sum(p, axis=0, keepdims=True)
            acc_ref[mp] = alpha * acc_ref[mp] + jnp.dot(vt, p.astype(BF16), preferred_element_type=F32)
            m_ref[mp] = m_new

    key_minus_query = (lax.broadcasted_iota(jnp.int32, (K_TILE, Q_TILE), 0)
                       - lax.broadcasted_iota(jnp.int32, (K_TILE, Q_TILE), 1))
    corr = (-2.0 * LOG2_E * slopes_ref[head]) * jnp.maximum(key_minus_query, 0).astype(F32)
    scores(0, buf0, corr)

    def pair(p, carry):
        n = 2 * p
        scores(n + 1, buf1)
        absorb(n, buf0)
        scores(n + 2, buf0)
        absorb(n + 1, buf1)
        return carry

    lax.fori_loop(0, (nk - 2) // 2, pair, 0)
    scores(nk - 1, buf1)
    absorb(nk - 2, buf0)
    absorb(nk - 1, buf1)

    lv = lam_ref[...]
    lam = (jnp.exp(jnp.sum(lv[0:1] * lv[1:2], axis=-1, keepdims=True))
           - jnp.exp(jnp.sum(lv[2:3] * lv[3:4], axis=-1, keepdims=True)) + lambda_init)
    o = acc_ref[0] / l_ref[0] - lam * (acc_ref[1] / l_ref[1])
    y = o * lax.rsqrt(jnp.mean(o * o, axis=0, keepdims=True) + SUBLN_EPS) * g_ref[...]
    o_ref[...] = (y * (1.0 - lambda_init)).T.astype(BF16)


def _attention(slopes, lam, qa, qb, ka, kb, vt, g_col, *, batch, seq, lambda_init):
    assert Q_TILE == K_TILE and (seq // K_TILE) % 2 == 0
    assert 8 % ATT_HEADS == 0
    t = batch * seq
    nq = seq // Q_TILE
    kv_rows = lambda b, h, i: (b, h)
    q_cols = lambda b, h, i: (h, b * nq + i)
    return pl.pallas_call(
        functools.partial(_attn_kernel, seq=seq, lambda_init=lambda_init),
        grid=(batch, ATT_HEADS, nq),
        in_specs=[
            pl.BlockSpec(memory_space=pltpu.SMEM),
            pl.BlockSpec((4, HEAD_DIM), lambda b, h, i: (0, 0)),
            pl.BlockSpec((V_DIM, Q_TILE), q_cols),
            pl.BlockSpec((V_DIM, Q_TILE), q_cols),
            pl.BlockSpec((seq, V_DIM), kv_rows),
            pl.BlockSpec((seq, V_DIM), kv_rows),
            pl.BlockSpec((V_DIM, seq), lambda b, h, i: (h, b)),
            pl.BlockSpec((V_DIM, 1), lambda b, h, i: (0, 0)),
        ],
        out_specs=pl.BlockSpec((Q_TILE, V_DIM), lambda b, h, i: (b * nq + i, h)),
        out_shape=jax.ShapeDtypeStruct((t, D_ATT), BF16),
        scratch_shapes=[
            pltpu.VMEM((2, 1, Q_TILE), F32),
            pltpu.VMEM((2, 1, Q_TILE), F32),
            pltpu.VMEM((2, V_DIM, Q_TILE), F32),
            pltpu.VMEM((2, K_TILE, Q_TILE), F32),
            pltpu.VMEM((2, K_TILE, Q_TILE), F32),
            pltpu.VMEM((2, 1, Q_TILE), F32),
            pltpu.VMEM((2, 1, Q_TILE), F32),
        ],
        compiler_params=pltpu.CompilerParams(
            dimension_semantics=("arbitrary", "arbitrary", "arbitrary"), vmem_limit_bytes=VMEM_LIMIT),
        name="diff_attention",
    )(slopes, lam, qa, qb, ka, kb, vt, g_col)


def _out_proj_kernel(b_ref, ch_ref, chp_ref, chn_ref, ya_ref, cw_ref, gc_ref, gmat_ref, w_ref,
                     x_ref, gp_ref, o_ref, *, tiles_per_seq):
    i = pl.program_id(0)
    has_prev = (i % tiles_per_seq != 0).astype(F32)
    has_next = (i % tiles_per_seq != tiles_per_seq - 1).astype(F32)
    ch = ch_ref[...]
    rows = lax.broadcasted_iota(jnp.int32, ch.shape, 0)
    prev = jnp.where(rows == 0, chp_ref[CONV_HALO - 1:CONV_HALO, :] * has_prev,
                     pltpu.roll(ch, 1, 0))
    nxt = jnp.where(rows == ROW_TILE - 1, chn_ref[0:1, :] * has_next,
                    pltpu.roll(ch, ROW_TILE - 1, 0))
    cw = cw_ref[...]
    y = b_ref[...] * (prev * cw[0:1] + ch * cw[1:2] + nxt * cw[2:3])
    z = y * y
    z_hi = z.astype(BF16)
    z_lo = (z - z_hi.astype(F32)).astype(BF16)
    gmat = gmat_ref[...]
    ms = (jnp.dot(z_hi, gmat, preferred_element_type=F32)
          + jnp.dot(z_lo, gmat, preferred_element_type=F32))
    yc = (y * lax.rsqrt(ms + NORM_EPS) * gc_ref[...]).astype(BF16)
    mix = (jnp.dot(yc, w_ref[0:D_CONV, :], preferred_element_type=F32)
           + jnp.dot(ya_ref[...], w_ref[D_CONV:, :], preferred_element_type=F32))
    o_ref[...] = x_ref[...] + _rms_norm(mix, gp_ref[...], NORM_EPS)


def _out_proj(b, ch, ya, cw, gc, gmat, w, x2, gp, *, seq):
    t = x2.shape[0]
    n = t // ROW_TILE
    hb = ROW_TILE // CONV_HALO
    n_halo = t // CONV_HALO
    row = lambda i: (i, 0)
    const = lambda i: (0, 0)
    return pl.pallas_call(
        functools.partial(_out_proj_kernel, tiles_per_seq=seq // ROW_TILE),
        grid=(n,),
        in_specs=[
            pl.BlockSpec((ROW_TILE, D_CONV), row),
            pl.BlockSpec((ROW_TILE, D_CONV), row),
            pl.BlockSpec((CONV_HALO, D_CONV), lambda i: (jnp.maximum(i * hb - 1, 0), 0)),
            pl.BlockSpec((CONV_HALO, D_CONV), lambda i: (jnp.minimum((i + 1) * hb, n_halo - 1), 0)),
            pl.BlockSpec((ROW_TILE, D_ATT), row),
            pl.BlockSpec((3, D_CONV), const),
            pl.BlockSpec((1, D_CONV), const),
            pl.BlockSpec((D_CONV, D_CONV), const),
            pl.BlockSpec((D_MODEL, D_MODEL), const, pipeline_mode=pl.Buffered(1)),
            pl.BlockSpec((ROW_TILE, D_MODEL), row),
            pl.BlockSpec((1, D_MODEL), const),
        ],
        out_specs=pl.BlockSpec((ROW_TILE, D_MODEL), row),
        out_shape=jax.ShapeDtypeStruct((t, D_MODEL), F32),
        compiler_params=pltpu.CompilerParams(
            dimension_semantics=("arbitrary",), vmem_limit_bytes=VMEM_LIMIT),
        name="out_proj",
    )(b, ch, ch, ch, ya, cw, gc, gmat, w, x2, gp)


def _ffn_kernel(x_ref, xp_ref, xn_ref, g_ref, wu_ref, cw_ref, cb_ref, wd_ref, gp_ref, o_ref,
                hs_ref, act_ref, *, tiles_per_seq):
    i = pl.program_id(0)
    has_prev = (i % tiles_per_seq != 0).astype(F32)
    has_next = (i % tiles_per_seq != tiles_per_seq - 1).astype(F32)
    g = g_ref[...]
    x = x_ref[...]
    hs_ref[0:FFN_HALO, :] = (_rms_norm(xp_ref[...], g, NORM_EPS) * has_prev).astype(BF16)
    hs_ref[FFN_HALO:FFN_HALO + ROW_TILE, :] = _rms_norm(x, g, NORM_EPS).astype(BF16)
    hs_ref[FFN_HALO + ROW_TILE:, :] = (_rms_norm(xn_ref[...], g, NORM_EPS) * has_next).astype(BF16)
    hs = hs_ref[...]
    ext = ROW_TILE + 2 * FFN_HALO

    def conv_chunk(c0):
        cols = slice(c0, c0 + FFN_CHUNK)
        u = jnp.dot(hs, wu_ref[:, cols], preferred_element_type=F32)
        cw = cw_ref[:, cols]
        mid = slice(FFN_HALO, FFN_HALO + ROW_TILE)
        return (pltpu.roll(u, 1, 0)[mid] * cw[0:1] + u[mid] * cw[1:2]
                + pltpu.roll(u, ext - 1, 0)[mid] * cw[2:3] + cb_ref[:, cols])

    for c in range(D_FF // FFN_CHUNK):
        gate = conv_chunk(c * FFN_CHUNK)
        up = conv_chunk(D_FF + c * FFN_CHUNK)
        act = gate * (1.0 / (1.0 + jnp.exp(-gate))) * up
        act_ref[:, c * FFN_CHUNK:(c + 1) * FFN_CHUNK] = act.astype(BF16)

    f = jnp.dot(act_ref[...], wd_ref[...], preferred_element_type=F32)
    o_ref[...] = x + _rms_norm(f, gp_ref[...], NORM_EPS)


def _ffn(x1, g, wu, cw, cb, wd, gp, *, seq):
    t = x1.shape[0]
    n = t // ROW_TILE
    hb = ROW_TILE // FFN_HALO
    n_halo = t // FFN_HALO
    row = lambda i: (i, 0)
    const = lambda i: (0, 0)
    return pl.pallas_call(
        functools.partial(_ffn_kernel, tiles_per_seq=seq // ROW_TILE),
        grid=(n,),
        in_specs=[
            pl.BlockSpec((ROW_TILE, D_MODEL), row),
            pl.BlockSpec((FFN_HALO, D_MODEL), lambda i: (jnp.maximum(i * hb - 1, 0), 0)),
            pl.BlockSpec((FFN_HALO, D_MODEL), lambda i: (jnp.minimum((i + 1) * hb, n_halo - 1), 0)),
            pl.BlockSpec((1, D_MODEL), const),
            pl.BlockSpec((D_MODEL, 2 * D_FF), const, pipeline_mode=pl.Buffered(1)),
            pl.BlockSpec((3, 2 * D_FF), const),
            pl.BlockSpec((1, 2 * D_FF), const),
            pl.BlockSpec((D_FF, D_MODEL), const, pipeline_mode=pl.Buffered(1)),
            pl.BlockSpec((1, D_MODEL), const),
        ],
        out_specs=pl.BlockSpec((ROW_TILE, D_MODEL), row),
        out_shape=jax.ShapeDtypeStruct((t, D_MODEL), F32),
        scratch_shapes=[
            pltpu.VMEM((ROW_TILE + 2 * FFN_HALO, D_MODEL), BF16),
            pltpu.VMEM((ROW_TILE, D_FF), BF16),
        ],
        compiler_params=pltpu.CompilerParams(
            dimension_semantics=("arbitrary",), vmem_limit_bytes=VMEM_LIMIT),
        name="conv_glu_ffn",
    )(x1, x1, x1, g, wu, cw, cb, wd, gp)


def kernel(x, g_mix_pre, w_mix_in, conv_w, g_conv_out, lambda_q1, lambda_k1, lambda_q2, lambda_k2,
           g_subln, w_mix_out, g_mix_post, g_ffn_pre, w_ffn_up, ffn_conv_w, ffn_conv_b, w_ffn_down,
           g_ffn_post):
    batch, seq, d = x.shape
    depth = g_mix_pre.shape[0]
    assert d == D_MODEL and seq % ROW_TILE == 0 and seq % Q_TILE == 0 and seq % K_TILE == 0
    assert seq <= POS_SPLIT * POS_SPLIT
    t = batch * seq
    x2 = x.reshape(t, d)
    slopes = 2.0 ** (-8.0 * (jnp.arange(ATT_HEADS, dtype=F32) + 1.0) / ATT_HEADS)
    coef = _position_coefficients()
    group = jnp.arange(D_CONV) // CONV_GROUP_DIM
    gmat = jnp.where(group[:, None] == group[None, :], 1.0 / CONV_GROUP_DIM, 0.0).astype(BF16)

    for layer in range(depth):
        lambda_init = 0.8 - 0.6 * math.exp(-0.3 * layer)
        lam = jnp.stack([lambda_q1[layer], lambda_k1[layer], lambda_q2[layer], lambda_k2[layer]]).astype(F32)
        b, ch, ka, kb, qa, qb, vt = _in_proj(x2, g_mix_pre[layer][None], coef,
                                             w_mix_in[layer].astype(BF16), seq=seq)
        ya = _attention(slopes, lam, qa, qb, ka, kb, vt, g_subln[layer][:, None],
                        batch=batch, seq=seq, lambda_init=lambda_init)
        x2 = _out_proj(b, ch, ya, conv_w[layer], g_conv_out[layer][None], gmat,
                       w_mix_out[layer].astype(BF16), x2, g_mix_post[layer][None], seq=seq)
        x2 = _ffn(x2, g_ffn_pre[layer][None], w_ffn_up[layer].astype(BF16), ffn_conv_w[layer],
                  ffn_conv_b[layer][None], w_ffn_down[layer].astype(BF16), g_ffn_post[layer][None],
                  seq=seq)
    return x2.reshape(batch, seq, d)
```

```python
import functools
import math

import jax
import jax.numpy as jnp
import numpy as np
from jax import lax
from jax.experimental import pallas as pl
from jax.experimental.pallas import tpu as pltpu

F32 = jnp.float32
BF16 = jnp.bfloat16

D_MODEL = 1024
D_CONV = 512
D_ATT = 512
CONV_GROUP_DIM = 64
HEAD_DIM = 64
V_DIM = 2 * HEAD_DIM
SUM_ROWS = 16
V_AUG = V_DIM + SUM_ROWS
ATT_HEADS = D_ATT // V_DIM
D_MIX_IN = 3 * D_CONV + 3 * D_ATT
D_FF = 2816
NORM_EPS = 1e-6
SUBLN_EPS = 1e-5
QK_SCALE = HEAD_DIM ** -0.5
LOG2_E = math.log2(math.e)

ROW_TILE = 512
Q_TILE = 512
K_TILE = 512
FFN_HALO = 16
CONV_HALO = 8
FFN_CHUNK = 256
POS_SPLIT = 64
VMEM_LIMIT = 56 * 1024 * 1024


def _rms_norm(x, g, eps):
    return x * lax.rsqrt(jnp.mean(x * x, axis=-1, keepdims=True) + eps) * g


def _in_proj_kernel(x_ref, g_ref, coef_ref, w_ref, b_ref, ch_ref, ka_ref, kb_ref, qa_ref, qb_ref,
                    vt_ref, *, tiles_per_seq):
    h = _rms_norm(x_ref[...], g_ref[...], NORM_EPS).astype(BF16)

    def proj(col):
        return jnp.dot(h, w_ref[:, col * D_CONV:(col + 1) * D_CONV], preferred_element_type=F32)

    b_ref[...] = proj(0)
    ch_ref[...] = proj(1) * proj(2)

    pos = ((pl.program_id(0) % tiles_per_seq) * ROW_TILE
           + lax.broadcasted_iota(jnp.int32, (ROW_TILE, 1), 0))
    hi = (pos // POS_SPLIT).astype(F32)
    lo = (pos % POS_SPLIT).astype(F32)
    coef = coef_ref[...]
    k_pos = hi * coef[0:1] + lo * coef[1:2] + coef[2:3]
    q_pos = hi * coef[3:4] + lo * coef[4:5] + coef[5:6]
    lane = lax.broadcasted_iota(jnp.int32, (ROW_TILE, D_ATT), 1)
    first_map = (lane & HEAD_DIM) == 0
    q = proj(3) * (QK_SCALE * LOG2_E)
    qa_ref[...] = jnp.where(first_map, q, q_pos).T.astype(BF16)
    qb_ref[...] = jnp.where(first_map, q_pos, q).T.astype(BF16)
    k = proj(4)
    ka_ref[...] = jnp.where(first_map, k, k_pos).astype(BF16)
    kb_ref[...] = jnp.where(first_map, k_pos, k).astype(BF16)
    vt = proj(5).T.astype(BF16)
    ones_row = (lax.broadcasted_iota(jnp.int32, (SUM_ROWS, ROW_TILE), 0) == 0).astype(BF16)
    for hd in range(ATT_HEADS):
        vt_ref[hd * V_AUG:hd * V_AUG + V_DIM, :] = vt[hd * V_DIM:(hd + 1) * V_DIM]
        vt_ref[hd * V_AUG + V_DIM:(hd + 1) * V_AUG, :] = ones_row


def _position_coefficients():
    lane = jnp.arange(D_ATT)
    e = lane % HEAD_DIM
    slope = 2.0 ** (-8.0 * ((lane // V_DIM).astype(F32) + 1.0) / ATT_HEADS)
    pieces = []
    rest = np.float64(LOG2_E)
    for _ in range(3):
        piece = np.float64(rest.astype(BF16))
        pieces.append(piece)
        rest = rest - piece
    log2e_piece = jnp.asarray(pieces, F32)[e % 3]
    zero = jnp.zeros((D_ATT,), F32)
    rows = [
        jnp.where(e < 3, slope * POS_SPLIT, 0.0),
        jnp.where((e >= 3) & (e < 6), slope, 0.0),
        jnp.where((e >= 6) & (e < 12), log2e_piece, 0.0),
        jnp.where((e >= 6) & (e < 9), -slope * POS_SPLIT, 0.0),
        jnp.where((e >= 9) & (e < 12), -slope, 0.0),
        jnp.where(e < 6, log2e_piece, 0.0),
        zero, zero,
    ]
    return jnp.stack(rows).astype(F32)


def _in_proj(x2, g, coef, w, *, seq):
    t = x2.shape[0]
    grid = (t // ROW_TILE,)
    row = lambda i: (i, 0)
    col = lambda i: (0, i)
    const = lambda i: (0, 0)
    return pl.pallas_call(
        functools.partial(_in_proj_kernel, tiles_per_seq=seq // ROW_TILE),
        grid=grid,
        in_specs=[
            pl.BlockSpec((ROW_TILE, D_MODEL), row),
            pl.BlockSpec((1, D_MODEL), const),
            pl.BlockSpec((8, D_ATT), const),
            pl.BlockSpec((D_MODEL, D_MIX_IN), const, pipeline_mode=pl.Buffered(1)),
        ],
        out_specs=[
            pl.BlockSpec((ROW_TILE, D_CONV), row),
            pl.BlockSpec((ROW_TILE, D_CONV), row),
            pl.BlockSpec((ROW_TILE, D_ATT), row),
            pl.BlockSpec((ROW_TILE, D_ATT), row),
            pl.BlockSpec((D_ATT, ROW_TILE), col),
            pl.BlockSpec((D_ATT, ROW_TILE), col),
            pl.BlockSpec((ATT_HEADS * V_AUG, ROW_TILE), col),
        ],
        out_shape=[
            jax.ShapeDtypeStruct((t, D_CONV), F32),
            jax.ShapeDtypeStruct((t, D_CONV), F32),
            jax.ShapeDtypeStruct((t, D_ATT), BF16),
            jax.ShapeDtypeStruct((t, D_ATT), BF16),
            jax.ShapeDtypeStruct((D_ATT, t), BF16),
            jax.ShapeDtypeStruct((D_ATT, t), BF16),
            jax.ShapeDtypeStruct((ATT_HEADS * V_AUG, t), BF16),
        ],
        compiler_params=pltpu.CompilerParams(
            dimension_semantics=("arbitrary",), vmem_limit_bytes=VMEM_LIMIT),
        name="in_proj",
    )(x2, g, coef, w)


def _attn_kernel(slopes_ref, lam_ref, qa_ref, qb_ref, ka_ref, kb_ref, vt_ref, g_ref, o_ref,
                 m_ref, acc_ref, s0_ref, s1_ref, smax0_ref, smax1_ref, *, seq, lambda_init):
    head = pl.program_id(1)
    qi = pl.program_id(2)
    nk = seq // K_TILE

    row = lax.broadcasted_iota(jnp.int32, (V_DIM, Q_TILE), 0)
    qa_before = qa_ref[...]
    qb_before = qb_ref[...]
    qa_after = jnp.where(row >= HEAD_DIM, -qa_before, qa_before)
    qb_after = jnp.where(row < HEAD_DIM, -qb_before, qb_before)

    m_ref[...] = jnp.full(m_ref.shape, -1e30, F32)
    acc_ref[...] = jnp.zeros(acc_ref.shape, F32)

    def tile_of(n):
        return jnp.where(n == 0, qi, jnp.where(n <= qi, n - 1, n))

    buf0 = (s0_ref, smax0_ref)
    buf1 = (s1_ref, smax1_ref)

    def scores(n, buf, corr=None):
        s_ref, smax_ref = buf
        j = tile_of(n)
        k0 = pl.multiple_of(j * K_TILE, K_TILE)
        keys_before = j <= qi
        for mp, (k_ref, q_before, q_after) in enumerate(
                ((ka_ref, qa_before, qa_after), (kb_ref, qb_before, qb_after))):
            qt = jnp.where(keys_before, q_before, q_after)
            s = jnp.dot(k_ref[pl.ds(k0, K_TILE), :], qt, preferred_element_type=F32)
            if corr is not None:
                s = s + corr
            s_ref[mp] = s
            smax_ref[mp] = jnp.max(s, axis=0, keepdims=True)

    def absorb(n, buf):
        s_ref, smax_ref = buf
        k0 = pl.multiple_of(tile_of(n) * K_TILE, K_TILE)
        vt = vt_ref[:, pl.ds(k0, K_TILE)]
        for mp in range(2):
            s = s_ref[mp]
            m_old = m_ref[mp]
            m_new = jnp.maximum(m_old, smax_ref[mp])
            alpha = jnp.exp2(m_old - m_new)
            p = jnp.exp2(s - m_new)
            acc_ref[mp] = alpha * acc_ref[mp] + jnp.dot(vt, p.astype(BF16), preferred_element_type=F32)
            m_ref[mp] = m_new

    key_minus_query = (lax.broadcasted_iota(jnp.int32, (K_TILE, Q_TILE), 0)
                       - lax.broadcasted_iota(jnp.int32, (K_TILE, Q_TILE), 1))
    corr = (-2.0 * LOG2_E * slopes_ref[head]) * jnp.maximum(key_minus_query, 0).astype(F32)
    scores(0, buf0, corr)

    def pair(p, carry):
        n = 2 * p
        scores(n + 1, buf1)
        absorb(n, buf0)
        scores(n + 2, buf0)
        absorb(n + 1, buf1)
        return carry

    lax.fori_loop(0, (nk - 2) // 2, pair, 0)
    scores(nk - 1, buf1)
    absorb(nk - 2, buf0)
    absorb(nk - 1, buf1)

    lv = lam_ref[...]
    lam = (jnp.exp(jnp.sum(lv[0:1] * lv[1:2], axis=-1, keepdims=True))
           - jnp.exp(jnp.sum(lv[2:3] * lv[3:4], axis=-1, keepdims=True)) + lambda_init)
    def normalised(mp):
        a = acc_ref[mp]
        return a[:V_DIM] / a[V_DIM:V_DIM + 1]

    o = normalised(0) - lam * normalised(1)
    y = o * lax.rsqrt(jnp.mean(o * o, axis=0, keepdims=True) + SUBLN_EPS) * g_ref[...]
    o_ref[...] = (y * (1.0 - lambda_init)).T.astype(BF16)


def _attention(slopes, lam, qa, qb, ka, kb, vt, g_col, *, batch, seq, lambda_init):
    assert Q_TILE == K_TILE and (seq // K_TILE) % 2 == 0
    assert 8 % ATT_HEADS == 0
    t = batch * seq
    nq = seq // Q_TILE
    kv_rows = lambda b, h, i: (b, h)
    q_cols = lambda b, h, i: (h, b * nq + i)
    return pl.pallas_call(
        functools.partial(_attn_kernel, seq=seq, lambda_init=lambda_init),
        grid=(batch, ATT_HEADS, nq),
        in_specs=[
            pl.BlockSpec(memory_space=pltpu.SMEM),
            pl.BlockSpec((4, HEAD_DIM), lambda b, h, i: (0, 0)),
            pl.BlockSpec((V_DIM, Q_TILE), q_cols),
            pl.BlockSpec((V_DIM, Q_TILE), q_cols),
            pl.BlockSpec((seq, V_DIM), kv_rows),
            pl.BlockSpec((seq, V_DIM), kv_rows),
            pl.BlockSpec((V_AUG, seq), lambda b, h, i: (h, b)),
            pl.BlockSpec((V_DIM, 1), lambda b, h, i: (0, 0)),
        ],
        out_specs=pl.BlockSpec((Q_TILE, V_DIM), lambda b, h, i: (b * nq + i, h)),
        out_shape=jax.ShapeDtypeStruct((t, D_ATT), BF16),
        scratch_shapes=[
            pltpu.VMEM((2, 1, Q_TILE), F32),
            pltpu.VMEM((2, V_AUG, Q_TILE), F32),
            pltpu.VMEM((2, K_TILE, Q_TILE), F32),
            pltpu.VMEM((2, K_TILE, Q_TILE), F32),
            pltpu.VMEM((2, 1, Q_TILE), F32),
            pltpu.VMEM((2, 1, Q_TILE), F32),
        ],
        compiler_params=pltpu.CompilerParams(
            dimension_semantics=("arbitrary", "arbitrary", "arbitrary"), vmem_limit_bytes=VMEM_LIMIT),
        name="diff_attention",
    )(slopes, lam, qa, qb, ka, kb, vt, g_col)


def _out_proj_kernel(b_ref, ch_ref, chp_ref, chn_ref, ya_ref, cw_ref, gc_ref, gmat_ref, w_ref,
                     x_ref, gp_ref, o_ref, *, tiles_per_seq):
    i = pl.program_id(0)
    has_prev = (i % tiles_per_seq != 0).astype(F32)
    has_next = (i % tiles_per_seq != tiles_per_seq - 1).astype(F32)
    ch = ch_ref[...]
    rows = lax.broadcasted_iota(jnp.int32, ch.shape, 0)
    prev = jnp.where(rows == 0, chp_ref[CONV_HALO - 1:CONV_HALO, :] * has_prev,
                     pltpu.roll(ch, 1, 0))
    nxt = jnp.where(rows == ROW_TILE - 1, chn_ref[0:1, :] * has_next,
                    pltpu.roll(ch, ROW_TILE - 1, 0))
    cw = cw_ref[...]
    y = b_ref[...] * (prev * cw[0:1] + ch * cw[1:2] + nxt * cw[2:3])
    z = y * y
    z_hi = z.astype(BF16)
    z_lo = (z - z_hi.astype(F32)).astype(BF16)
    gmat = gmat_ref[...]
    ms = (jnp.dot(z_hi, gmat, preferred_element_type=F32)
          + jnp.dot(z_lo, gmat, preferred_element_type=F32))
    yc = (y * lax.rsqrt(ms + NORM_EPS) * gc_ref[...]).astype(BF16)
    mix = (jnp.dot(yc, w_ref[0:D_CONV, :], preferred_element_type=F32)
           + jnp.dot(ya_ref[...], w_ref[D_CONV:, :], preferred_element_type=F32))
    o_ref[...] = x_ref[...] + _rms_norm(mix, gp_ref[...], NORM_EPS)


def _out_proj(b, ch, ya, cw, gc, gmat, w, x2, gp, *, seq):
    t = x2.shape[0]
    n = t // ROW_TILE
    hb = ROW_TILE // CONV_HALO
    n_halo = t // CONV_HALO
    row = lambda i: (i, 0)
    const = lambda i: (0, 0)
    return pl.pallas_call(
        functools.partial(_out_proj_kernel, tiles_per_seq=seq // ROW_TILE),
        grid=(n,),
        in_specs=[
            pl.BlockSpec((ROW_TILE, D_CONV), row),
            pl.BlockSpec((ROW_TILE, D_CONV), row),
            pl.BlockSpec((CONV_HALO, D_CONV), lambda i: (jnp.maximum(i * hb - 1, 0), 0)),
            pl.BlockSpec((CONV_HALO, D_CONV), lambda i: (jnp.minimum((i + 1) * hb, n_halo - 1), 0)),
            pl.BlockSpec((ROW_TILE, D_ATT), row),
            pl.BlockSpec((3, D_CONV), const),
            pl.BlockSpec((1, D_CONV), const),
            pl.BlockSpec((D_CONV, D_CONV), const),
            pl.BlockSpec((D_MODEL, D_MODEL), const, pipeline_mode=pl.Buffered(1)),
            pl.BlockSpec((ROW_TILE, D_MODEL), row),
            pl.BlockSpec((1, D_MODEL), const),
        ],
        out_specs=pl.BlockSpec((ROW_TILE, D_MODEL), row),
        out_shape=jax.ShapeDtypeStruct((t, D_MODEL), F32),
        compiler_params=pltpu.CompilerParams(
            dimension_semantics=("arbitrary",), vmem_limit_bytes=VMEM_LIMIT),
        name="out_proj",
    )(b, ch, ch, ch, ya, cw, gc, gmat, w, x2, gp)


def _ffn_kernel(x_ref, xp_ref, xn_ref, g_ref, wu_ref, cw_ref, cb_ref, wd_ref, gp_ref, o_ref,
                hs_ref, act_ref, *, tiles_per_seq):
    i = pl.program_id(0)
    has_prev = (i % tiles_per_seq != 0).astype(F32)
    has_next = (i % tiles_per_seq != tiles_per_seq - 1).astype(F32)
    g = g_ref[...]
    x = x_ref[...]
    hs_ref[0:FFN_HALO, :] = (_rms_norm(xp_ref[...], g, NORM_EPS) * has_prev).astype(BF16)
    hs_ref[FFN_HALO:FFN_HALO + ROW_TILE, :] = _rms_norm(x, g, NORM_EPS).astype(BF16)
    hs_ref[FFN_HALO + ROW_TILE:, :] = (_rms_norm(xn_ref[...], g, NORM_EPS) * has_next).astype(BF16)
    hs = hs_ref[...]
    ext = ROW_TILE + 2 * FFN_HALO

    def conv_chunk(c0):
        cols = slice(c0, c0 + FFN_CHUNK)
        u = jnp.dot(hs, wu_ref[:, cols], preferred_element_type=F32)
        cw = cw_ref[:, cols]
        mid = slice(FFN_HALO, FFN_HALO + ROW_TILE)
        return (pltpu.roll(u, 1, 0)[mid] * cw[0:1] + u[mid] * cw[1:2]
                + pltpu.roll(u, ext - 1, 0)[mid] * cw[2:3] + cb_ref[:, cols])

    for c in range(D_FF // FFN_CHUNK):
        gate = conv_chunk(c * FFN_CHUNK)
        up = conv_chunk(D_FF + c * FFN_CHUNK)
        act = gate * (1.0 / (1.0 + jnp.exp(-gate))) * up
        act_ref[:, c * FFN_CHUNK:(c + 1) * FFN_CHUNK] = act.astype(BF16)

    f = jnp.dot(act_ref[...], wd_ref[...], preferred_element_type=F32)
    o_ref[...] = x + _rms_norm(f, gp_ref[...], NORM_EPS)


def _ffn(x1, g, wu, cw, cb, wd, gp, *, seq):
    t = x1.shape[0]
    n = t // ROW_TILE
    hb = ROW_TILE // FFN_HALO
    n_halo = t // FFN_HALO
    row = lambda i: (i, 0)
    const = lambda i: (0, 0)
    return pl.pallas_call(
        functools.partial(_ffn_kernel, tiles_per_seq=seq // ROW_TILE),
        grid=(n,),
        in_specs=[
            pl.BlockSpec((ROW_TILE, D_MODEL), row),
            pl.BlockSpec((FFN_HALO, D_MODEL), lambda i: (jnp.maximum(i * hb - 1, 0), 0)),
            pl.BlockSpec((FFN_HALO, D_MODEL), lambda i: (jnp.minimum((i + 1) * hb, n_halo - 1), 0)),
            pl.BlockSpec((1, D_MODEL), const),
            pl.BlockSpec((D_MODEL, 2 * D_FF), const, pipeline_mode=pl.Buffered(1)),
            pl.BlockSpec((3, 2 * D_FF), const),
            pl.BlockSpec((1, 2 * D_FF), const),
            pl.BlockSpec((D_FF, D_MODEL), const, pipeline_mode=pl.Buffered(1)),
            pl.BlockSpec((1, D_MODEL), const),
        ],
        out_specs=pl.BlockSpec((ROW_TILE, D_MODEL), row),
        out_shape=jax.ShapeDtypeStruct((t, D_MODEL), F32),
        scratch_shapes=[
            pltpu.VMEM((ROW_TILE + 2 * FFN_HALO, D_MODEL), BF16),
            pltpu.VMEM((ROW_TILE, D_FF), BF16),
        ],
        compiler_params=pltpu.CompilerParams(
            dimension_semantics=("arbitrary",), vmem_limit_bytes=VMEM_LIMIT),
        name="conv_glu_ffn",
    )(x1, x1, x1, g, wu, cw, cb, wd, gp)


def kernel(x, g_mix_pre, w_mix_in, conv_w, g_conv_out, lambda_q1, lambda_k1, lambda_q2, lambda_k2,
           g_subln, w_mix_out, g_mix_post, g_ffn_pre, w_ffn_up, ffn_conv_w, ffn_conv_b, w_ffn_down,
           g_ffn_post):
    batch, seq, d = x.shape
    depth = g_mix_pre.shape[0]
    assert d == D_MODEL and seq % ROW_TILE == 0 and seq % Q_TILE == 0 and seq % K_TILE == 0
    assert seq <= POS_SPLIT * POS_SPLIT
    t = batch * seq
    x2 = x.reshape(t, d)
    slopes = 2.0 ** (-8.0 * (jnp.arange(ATT_HEADS, dtype=F32) + 1.0) / ATT_HEADS)
    coef = _position_coefficients()
    group = jnp.arange(D_CONV) // CONV_GROUP_DIM
    gmat = jnp.where(group[:, None] == group[None, :], 1.0 / CONV_GROUP_DIM, 0.0).astype(BF16)

    for layer in range(depth):
        lambda_init = 0.8 - 0.6 * math.exp(-0.3 * layer)
        lam = jnp.stack([lambda_q1[layer], lambda_k1[layer], lambda_q2[layer], lambda_k2[layer]]).astype(F32)
        b, ch, ka, kb, qa, qb, vt = _in_proj(x2, g_mix_pre[layer][None], coef,
                                             w_mix_in[layer].astype(BF16), seq=seq)
        ya = _attention(slopes, lam, qa, qb, ka, kb, vt, g_subln[layer][:, None],
                        batch=batch, seq=seq, lambda_init=lambda_init)
        x2 = _out_proj(b, ch, ya, conv_w[layer], g_conv_out[layer][None], gmat,
                       w_mix_out[layer].astype(BF16), x2, g_mix_post[layer][None], seq=seq)
        x2 = _ffn(x2, g_ffn_pre[layer][None], w_ffn_up[layer].astype(BF16), ffn_conv_w[layer],
                  ffn_conv_b[layer][None], w_ffn_down[layer].astype(BF16), g_ffn_post[layer][None],
                  seq=seq)
    return x2.reshape(batch, seq, d)
```

```python
import functools
import math

import jax
import jax.numpy as jnp
import numpy as np
from jax import lax
from jax.experimental import pallas as pl
from jax.experimental.pallas import tpu as pltpu

F32 = jnp.float32
BF16 = jnp.bfloat16

D_MODEL = 1024
D_CONV = 512
D_ATT = 512
CONV_GROUP_DIM = 64
HEAD_DIM = 64
V_DIM = 2 * HEAD_DIM
SUM_ROWS = 16
V_AUG = V_DIM + SUM_ROWS
ATT_HEADS = D_ATT // V_DIM
D_MIX_IN = 3 * D_CONV + 3 * D_ATT
D_FF = 2816
NORM_EPS = 1e-6
SUBLN_EPS = 1e-5
QK_SCALE = HEAD_DIM ** -0.5
LOG2_E = math.log2(math.e)

ROW_TILE = 512
Q_TILE = 512
K_TILE = 512
FFN_HALO = 16
CONV_HALO = 8
FFN_CHUNK = 256
POS_SPLIT = 64
UNDERFLOW_LOG2 = 152.0
VMEM_LIMIT = 56 * 1024 * 1024


def _rms_norm(x, g, eps):
    return x * lax.rsqrt(jnp.mean(x * x, axis=-1, keepdims=True) + eps) * g


def _in_proj_kernel(x_ref, g_ref, coef_ref, w_ref, b_ref, ch_ref, ka_ref, kb_ref, qa_ref, qb_ref,
                    vt_ref, *, tiles_per_seq):
    h = _rms_norm(x_ref[...], g_ref[...], NORM_EPS).astype(BF16)

    def proj(col):
        return jnp.dot(h, w_ref[:, col * D_CONV:(col + 1) * D_CONV], preferred_element_type=F32)

    b_ref[...] = proj(0)
    ch_ref[...] = proj(1) * proj(2)

    pos = ((pl.program_id(0) % tiles_per_seq) * ROW_TILE
           + lax.broadcasted_iota(jnp.int32, (ROW_TILE, 1), 0))
    hi = (pos // POS_SPLIT).astype(F32)
    lo = (pos % POS_SPLIT).astype(F32)
    coef = coef_ref[...]
    k_pos = hi * coef[0:1] + lo * coef[1:2] + coef[2:3]
    q_pos = hi * coef[3:4] + lo * coef[4:5] + coef[5:6]
    lane = lax.broadcasted_iota(jnp.int32, (ROW_TILE, D_ATT), 1)
    first_map = (lane & HEAD_DIM) == 0
    q = proj(3) * (QK_SCALE * LOG2_E)
    qa_ref[...] = jnp.where(first_map, q, q_pos).T.astype(BF16)
    qb_ref[...] = jnp.where(first_map, q_pos, q).T.astype(BF16)
    k = proj(4)
    ka_ref[...] = jnp.where(first_map, k, k_pos).astype(BF16)
    kb_ref[...] = jnp.where(first_map, k_pos, k).astype(BF16)
    vt = proj(5).T.astype(BF16)
    ones_row = (lax.broadcasted_iota(jnp.int32, (SUM_ROWS, ROW_TILE), 0) == 0).astype(BF16)
    for hd in range(ATT_HEADS):
        vt_ref[hd * V_AUG:hd * V_AUG + V_DIM, :] = vt[hd * V_DIM:(hd + 1) * V_DIM]
        vt_ref[hd * V_AUG + V_DIM:(hd + 1) * V_AUG, :] = ones_row


def _position_coefficients():
    lane = jnp.arange(D_ATT)
    e = lane % HEAD_DIM
    slope = 2.0 ** (-8.0 * ((lane // V_DIM).astype(F32) + 1.0) / ATT_HEADS)
    pieces = []
    rest = np.float64(LOG2_E)
    for _ in range(3):
        piece = np.float64(rest.astype(BF16))
        pieces.append(piece)
        rest = rest - piece
    log2e_piece = jnp.asarray(pieces, F32)[e % 3]
    zero = jnp.zeros((D_ATT,), F32)
    rows = [
        jnp.where(e < 3, slope * POS_SPLIT, 0.0),
        jnp.where((e >= 3) & (e < 6), slope, 0.0),
        jnp.where((e >= 6) & (e < 12), log2e_piece, 0.0),
        jnp.where((e >= 6) & (e < 9), -slope * POS_SPLIT, 0.0),
        jnp.where((e >= 9) & (e < 12), -slope, 0.0),
        jnp.where(e < 6, log2e_piece, 0.0),
        zero, zero,
    ]
    return jnp.stack(rows).astype(F32)


def _in_proj(x2, g, coef, w, *, seq):
    t = x2.shape[0]
    grid = (t // ROW_TILE,)
    row = lambda i: (i, 0)
    col = lambda i: (0, i)
    const = lambda i: (0, 0)
    return pl.pallas_call(
        functools.partial(_in_proj_kernel, tiles_per_seq=seq // ROW_TILE),
        grid=grid,
        in_specs=[
            pl.BlockSpec((ROW_TILE, D_MODEL), row),
            pl.BlockSpec((1, D_MODEL), const),
            pl.BlockSpec((8, D_ATT), const),
            pl.BlockSpec((D_MODEL, D_MIX_IN), const, pipeline_mode=pl.Buffered(1)),
        ],
        out_specs=[
            pl.BlockSpec((ROW_TILE, D_CONV), row),
            pl.BlockSpec((ROW_TILE, D_CONV), row),
            pl.BlockSpec((ROW_TILE, D_ATT), row),
            pl.BlockSpec((ROW_TILE, D_ATT), row),
            pl.BlockSpec((D_ATT, ROW_TILE), col),
            pl.BlockSpec((D_ATT, ROW_TILE), col),
            pl.BlockSpec((ATT_HEADS * V_AUG, ROW_TILE), col),
        ],
        out_shape=[
            jax.ShapeDtypeStruct((t, D_CONV), F32),
            jax.ShapeDtypeStruct((t, D_CONV), F32),
            jax.ShapeDtypeStruct((t, D_ATT), BF16),
            jax.ShapeDtypeStruct((t, D_ATT), BF16),
            jax.ShapeDtypeStruct((D_ATT, t), BF16),
            jax.ShapeDtypeStruct((D_ATT, t), BF16),
            jax.ShapeDtypeStruct((ATT_HEADS * V_AUG, t), BF16),
        ],
        compiler_params=pltpu.CompilerParams(
            dimension_semantics=("arbitrary",), vmem_limit_bytes=VMEM_LIMIT),
        name="in_proj",
    )(x2, g, coef, w)


def _attn_kernel(slopes_ref, lam_ref, qa_ref, qb_ref, ka_ref, kb_ref, vt_ref, g_ref, o_ref,
                 m_ref, acc_ref, s0_ref, s1_ref, smax0_ref, smax1_ref, corr_ref, reach_ref,
                 *, seq, lambda_init):
    head = pl.program_id(1)
    nk = seq // K_TILE
    decay = LOG2_E * slopes_ref[head]

    reach_ref[0] = nk - 1

    @pl.when(decay * (K_TILE * (nk - 2) + 1) > UNDERFLOW_LOG2)
    def _():
        def max_norm2(ref, feature_axis, first_half):
            x = ref[...].astype(F32)
            idx = lax.broadcasted_iota(jnp.int32, x.shape, feature_axis)
            feature = (idx < HEAD_DIM) if first_half else (idx >= HEAD_DIM)
            n2 = jnp.sum(jnp.where(feature, x * x, 0.0), axis=feature_axis, keepdims=True)
            return jnp.max(n2, axis=1 - feature_axis, keepdims=True)

        q2 = jnp.maximum(max_norm2(qa_ref, 0, True), max_norm2(qb_ref, 0, False))
        k2 = jnp.maximum(max_norm2(ka_ref, 1, True), max_norm2(kb_ref, 1, False))
        budget = 2.02 * jnp.sqrt(q2 * k2) + UNDERFLOW_LOG2
        reach = jnp.floor((budget / decay - 1.0) * (1.0 / K_TILE)) + 1.0
        reach = jnp.where(reach < nk - 1, jnp.maximum(reach, 0.0), nk - 1.0)
        reach_ref[0] = reach.astype(jnp.int32)[0, 0]

    reach = reach_ref[0]

    key_minus_query = (lax.broadcasted_iota(jnp.int32, (K_TILE, Q_TILE), 0)
                       - lax.broadcasted_iota(jnp.int32, (K_TILE, Q_TILE), 1))
    corr_ref[...] = (-2.0 * decay) * jnp.maximum(key_minus_query, 0).astype(F32)

    lv = lam_ref[...]
    lam = (jnp.exp(jnp.sum(lv[0:1] * lv[1:2], axis=-1, keepdims=True))
           - jnp.exp(jnp.sum(lv[2:3] * lv[3:4], axis=-1, keepdims=True)) + lambda_init)
    buf0 = (s0_ref, smax0_ref)
    buf1 = (s1_ref, smax1_ref)

    def query_tile(qi, carry):
        q0 = pl.multiple_of(qi * Q_TILE, Q_TILE)
        row = lax.broadcasted_iota(jnp.int32, (V_DIM, Q_TILE), 0)
        qa_before = qa_ref[:, pl.ds(q0, Q_TILE)]
        qb_before = qb_ref[:, pl.ds(q0, Q_TILE)]
        qa_after = jnp.where(row >= HEAD_DIM, -qa_before, qa_before)
        qb_after = jnp.where(row < HEAD_DIM, -qb_before, qb_before)

        m_ref[...] = jnp.full(m_ref.shape, -1e30, F32)
        acc_ref[...] = jnp.zeros(acc_ref.shape, F32)

        lo = jnp.maximum(qi - reach, 0)
        hi = jnp.minimum(qi + reach, nk - 1)
        odd = (hi - lo + 1) % 2
        grow_hi = jnp.where(hi < nk - 1, odd, 0)
        hi = hi + grow_hi
        lo = lo - (odd - grow_hi)
        visits = hi - lo + 1

        def tile_of(n):
            j = lo + n - 1
            return jnp.where(n == 0, qi, jnp.where(j < qi, j, j + 1))

        def scores(n, buf, diagonal=False):
            s_ref, smax_ref = buf
            j = tile_of(n)
            k0 = pl.multiple_of(j * K_TILE, K_TILE)
            keys_before = j <= qi
            for mp, (k_ref, q_before, q_after) in enumerate(
                    ((ka_ref, qa_before, qa_after), (kb_ref, qb_before, qb_after))):
                qt = jnp.where(keys_before, q_before, q_after)
                s = jnp.dot(k_ref[pl.ds(k0, K_TILE), :], qt, preferred_element_type=F32)
                if diagonal:
                    s = s + corr_ref[...]
                s_ref[mp] = s
                smax_ref[mp] = jnp.max(s, axis=0, keepdims=True)

        def absorb(n, buf):
            s_ref, smax_ref = buf
            k0 = pl.multiple_of(tile_of(n) * K_TILE, K_TILE)
            vt = vt_ref[:, pl.ds(k0, K_TILE)]
            for mp in range(2):
                s = s_ref[mp]
                m_old = m_ref[mp]
                m_new = jnp.maximum(m_old, smax_ref[mp])
                alpha = jnp.exp2(m_old - m_new)
                p = jnp.exp2(s - m_new)
                acc_ref[mp] = alpha * acc_ref[mp] + jnp.dot(vt, p.astype(BF16),
                                                            preferred_element_type=F32)
                m_ref[mp] = m_new

        scores(0, buf0, diagonal=True)

        def pair(p, c):
            n = 2 * p
            scores(n + 1, buf1)
            absorb(n, buf0)
            scores(n + 2, buf0)
            absorb(n + 1, buf1)
            return c

        lax.fori_loop(0, (visits - 2) // 2, pair, 0)
        scores(visits - 1, buf1)
        absorb(visits - 2, buf0)
        absorb(visits - 1, buf1)

        def normalised(mp):
            a = acc_ref[mp]
            return a[:V_DIM] / a[V_DIM:V_DIM + 1]

        o = normalised(0) - lam * normalised(1)
        y = o * lax.rsqrt(jnp.mean(o * o, axis=0, keepdims=True) + SUBLN_EPS) * g_ref[...]
        o_ref[pl.ds(q0, Q_TILE), :] = (y * (1.0 - lambda_init)).T.astype(BF16)
        return carry

    lax.fori_loop(0, seq // Q_TILE, query_tile, 0)


def _attention(slopes, lam, qa, qb, ka, kb, vt, g_col, *, batch, seq, lambda_init):
    assert Q_TILE == K_TILE and (seq // K_TILE) % 2 == 0
    assert 8 % ATT_HEADS == 0
    t = batch * seq
    rows = lambda b, h: (b, h)
    cols = lambda b, h: (h, b)
    const = lambda b, h: (0, 0)
    return pl.pallas_call(
        functools.partial(_attn_kernel, seq=seq, lambda_init=lambda_init),
        grid=(batch, ATT_HEADS),
        in_specs=[
            pl.BlockSpec(memory_space=pltpu.SMEM),
            pl.BlockSpec((4, HEAD_DIM), const),
            pl.BlockSpec((V_DIM, seq), cols),
            pl.BlockSpec((V_DIM, seq), cols),
            pl.BlockSpec((seq, V_DIM), rows),
            pl.BlockSpec((seq, V_DIM), rows),
            pl.BlockSpec((V_AUG, seq), cols),
            pl.BlockSpec((V_DIM, 1), const),
        ],
        out_specs=pl.BlockSpec((seq, V_DIM), rows),
        out_shape=jax.ShapeDtypeStruct((t, D_ATT), BF16),
        scratch_shapes=[
            pltpu.VMEM((2, 1, Q_TILE), F32),
            pltpu.VMEM((2, V_AUG, Q_TILE), F32),
            pltpu.VMEM((2, K_TILE, Q_TILE), F32),
            pltpu.VMEM((2, K_TILE, Q_TILE), F32),
            pltpu.VMEM((2, 1, Q_TILE), F32),
            pltpu.VMEM((2, 1, Q_TILE), F32),
            pltpu.VMEM((K_TILE, Q_TILE), F32),
            pltpu.SMEM((1,), jnp.int32),
        ],
        compiler_params=pltpu.CompilerParams(
            dimension_semantics=("arbitrary", "arbitrary"), vmem_limit_bytes=VMEM_LIMIT),
        name="diff_attention",
    )(slopes, lam, qa, qb, ka, kb, vt, g_col)


def _out_proj_kernel(b_ref, ch_ref, chp_ref, chn_ref, ya_ref, cw_ref, gc_ref, gmat_ref, w_ref,
                     x_ref, gp_ref, o_ref, *, tiles_per_seq):
    i = pl.program_id(0)
    has_prev = (i % tiles_per_seq != 0).astype(F32)
    has_next = (i % tiles_per_seq != tiles_per_seq - 1).astype(F32)
    ch = ch_ref[...]
    rows = lax.broadcasted_iota(jnp.int32, ch.shape, 0)
    prev = jnp.where(rows == 0, chp_ref[CONV_HALO - 1:CONV_HALO, :] * has_prev,
                     pltpu.roll(ch, 1, 0))
    nxt = jnp.where(rows == ROW_TILE - 1, chn_ref[0:1, :] * has_next,
                    pltpu.roll(ch, ROW_TILE - 1, 0))
    cw = cw_ref[...]
    y = b_ref[...] * (prev * cw[0:1] + ch * cw[1:2] + nxt * cw[2:3])
    z = y * y
    z_hi = z.astype(BF16)
    z_lo = (z - z_hi.astype(F32)).astype(BF16)
    gmat = gmat_ref[...]
    ms = (jnp.dot(z_hi, gmat, preferred_element_type=F32)
          + jnp.dot(z_lo, gmat, preferred_element_type=F32))
    yc = (y * lax.rsqrt(ms + NORM_EPS) * gc_ref[...]).astype(BF16)
    mix = (jnp.dot(yc, w_ref[0:D_CONV, :], preferred_element_type=F32)
           + jnp.dot(ya_ref[...], w_ref[D_CONV:, :], preferred_element_type=F32))
    o_ref[...] = x_ref[...] + _rms_norm(mix, gp_ref[...], NORM_EPS)


def _out_proj(b, ch, ya, cw, gc, gmat, w, x2, gp, *, seq):
    t = x2.shape[0]
    n = t // ROW_TILE
    hb = ROW_TILE // CONV_HALO
    n_halo = t // CONV_HALO
    row = lambda i: (i, 0)
    const = lambda i: (0, 0)
    return pl.pallas_call(
        functools.partial(_out_proj_kernel, tiles_per_seq=seq // ROW_TILE),
        grid=(n,),
        in_specs=[
            pl.BlockSpec((ROW_TILE, D_CONV), row),
            pl.BlockSpec((ROW_TILE, D_CONV), row),
            pl.BlockSpec((CONV_HALO, D_CONV), lambda i: (jnp.maximum(i * hb - 1, 0), 0)),
            pl.BlockSpec((CONV_HALO, D_CONV), lambda i: (jnp.minimum((i + 1) * hb, n_halo - 1), 0)),
            pl.BlockSpec((ROW_TILE, D_ATT), row),
            pl.BlockSpec((3, D_CONV), const),
            pl.BlockSpec((1, D_CONV), const),
            pl.BlockSpec((D_CONV, D_CONV), const),
            pl.BlockSpec((D_MODEL, D_MODEL), const, pipeline_mode=pl.Buffered(1)),
            pl.BlockSpec((ROW_TILE, D_MODEL), row),
            pl.BlockSpec((1, D_MODEL), const),
        ],
        out_specs=pl.BlockSpec((ROW_TILE, D_MODEL), row),
        out_shape=jax.ShapeDtypeStruct((t, D_MODEL), F32),
        compiler_params=pltpu.CompilerParams(
            dimension_semantics=("arbitrary",), vmem_limit_bytes=VMEM_LIMIT),
        name="out_proj",
    )(b, ch, ch, ch, ya, cw, gc, gmat, w, x2, gp)


def _ffn_kernel(x_ref, xp_ref, xn_ref, g_ref, wu_ref, cw_ref, cb_ref, wd_ref, gp_ref, o_ref,
                hs_ref, act_ref, *, tiles_per_seq):
    i = pl.program_id(0)
    has_prev = (i % tiles_per_seq != 0).astype(F32)
    has_next = (i % tiles_per_seq != tiles_per_seq - 1).astype(F32)
    g = g_ref[...]
    x = x_ref[...]
    hs_ref[0:FFN_HALO, :] = (_rms_norm(xp_ref[...], g, NORM_EPS) * has_prev).astype(BF16)
    hs_ref[FFN_HALO:FFN_HALO + ROW_TILE, :] = _rms_norm(x, g, NORM_EPS).astype(BF16)
    hs_ref[FFN_HALO + ROW_TILE:, :] = (_rms_norm(xn_ref[...], g, NORM_EPS) * has_next).astype(BF16)
    hs = hs_ref[...]
    ext = ROW_TILE + 2 * FFN_HALO

    def conv_chunk(c0):
        cols = slice(c0, c0 + FFN_CHUNK)
        u = jnp.dot(hs, wu_ref[:, cols], preferred_element_type=F32)
        cw = cw_ref[:, cols]
        mid = slice(FFN_HALO, FFN_HALO + ROW_TILE)
        return (pltpu.roll(u, 1, 0)[mid] * cw[0:1] + u[mid] * cw[1:2]
                + pltpu.roll(u, ext - 1, 0)[mid] * cw[2:3] + cb_ref[:, cols])

    for c in range(D_FF // FFN_CHUNK):
        gate = conv_chunk(c * FFN_CHUNK)
        up = conv_chunk(D_FF + c * FFN_CHUNK)
        act = gate * (1.0 / (1.0 + jnp.exp(-gate))) * up
        act_ref[:, c * FFN_CHUNK:(c + 1) * FFN_CHUNK] = act.astype(BF16)

    f = jnp.dot(act_ref[...], wd_ref[...], preferred_element_type=F32)
    o_ref[...] = x + _rms_norm(f, gp_ref[...], NORM_EPS)


def _ffn(x1, g, wu, cw, cb, wd, gp, *, seq):
    t = x1.shape[0]
    n = t // ROW_TILE
    hb = ROW_TILE // FFN_HALO
    n_halo = t // FFN_HALO
    row = lambda i: (i, 0)
    const = lambda i: (0, 0)
    return pl.pallas_call(
        functools.partial(_ffn_kernel, tiles_per_seq=seq // ROW_TILE),
        grid=(n,),
        in_specs=[
            pl.BlockSpec((ROW_TILE, D_MODEL), row),
            pl.BlockSpec((FFN_HALO, D_MODEL), lambda i: (jnp.maximum(i * hb - 1, 0), 0)),
            pl.BlockSpec((FFN_HALO, D_MODEL), lambda i: (jnp.minimum((i + 1) * hb, n_halo - 1), 0)),
            pl.BlockSpec((1, D_MODEL), const),
            pl.BlockSpec((D_MODEL, 2 * D_FF), const, pipeline_mode=pl.Buffered(1)),
            pl.BlockSpec((3, 2 * D_FF), const),
            pl.BlockSpec((1, 2 * D_FF), const),
            pl.BlockSpec((D_FF, D_MODEL), const, pipeline_mode=pl.Buffered(1)),
            pl.BlockSpec((1, D_MODEL), const),
        ],
        out_specs=pl.BlockSpec((ROW_TILE, D_MODEL), row),
        out_shape=jax.ShapeDtypeStruct((t, D_MODEL), F32),
        scratch_shapes=[
            pltpu.VMEM((ROW_TILE + 2 * FFN_HALO, D_MODEL), BF16),
            pltpu.VMEM((ROW_TILE, D_FF), BF16),
        ],
        compiler_params=pltpu.CompilerParams(
            dimension_semantics=("arbitrary",), vmem_limit_bytes=VMEM_LIMIT),
        name="conv_glu_ffn",
    )(x1, x1, x1, g, wu, cw, cb, wd, gp)


def kernel(x, g_mix_pre, w_mix_in, conv_w, g_conv_out, lambda_q1, lambda_k1, lambda_q2, lambda_k2,
           g_subln, w_mix_out, g_mix_post, g_ffn_pre, w_ffn_up, ffn_conv_w, ffn_conv_b, w_ffn_down,
           g_ffn_post):
    batch, seq, d = x.shape
    depth = g_mix_pre.shape[0]
    assert d == D_MODEL and seq % ROW_TILE == 0 and seq % Q_TILE == 0 and seq % K_TILE == 0
    assert seq <= POS_SPLIT * POS_SPLIT
    t = batch * seq
    x2 = x.reshape(t, d)
    slopes = 2.0 ** (-8.0 * (jnp.arange(ATT_HEADS, dtype=F32) + 1.0) / ATT_HEADS)
    coef = _position_coefficients()
    group = jnp.arange(D_CONV) // CONV_GROUP_DIM
    gmat = jnp.where(group[:, None] == group[None, :], 1.0 / CONV_GROUP_DIM, 0.0).astype(BF16)

    for layer in range(depth):
        lambda_init = 0.8 - 0.6 * math.exp(-0.3 * layer)
        lam = jnp.stack([lambda_q1[layer], lambda_k1[layer], lambda_q2[layer], lambda_k2[layer]]).astype(F32)
        b, ch, ka, kb, qa, qb, vt = _in_proj(x2, g_mix_pre[layer][None], coef,
                                             w_mix_in[layer].astype(BF16), seq=seq)
        ya = _attention(slopes, lam, qa, qb, ka, kb, vt, g_subln[layer][:, None],
                        batch=batch, seq=seq, lambda_init=lambda_init)
        x2 = _out_proj(b, ch, ya, conv_w[layer], g_conv_out[layer][None], gmat,
                       w_mix_out[layer].astype(BF16), x2, g_mix_post[layer][None], seq=seq)
        x2 = _ffn(x2, g_ffn_pre[layer][None], w_ffn_up[layer].astype(BF16), ffn_conv_w[layer],
                  ffn_conv_b[layer][None], w_ffn_down[layer].astype(BF16), g_ffn_post[layer][None],
                  seq=seq)
    return x2.reshape(batch, seq, d)
```

```python
import functools
import math

import jax
import jax.numpy as jnp
import numpy as np
from jax import lax
from jax.experimental import pallas as pl
from jax.experimental.pallas import tpu as pltpu

F32 = jnp.float32
BF16 = jnp.bfloat16

D_MODEL = 1024
D_CONV = 512
D_ATT = 512
CONV_GROUP_DIM = 64
HEAD_DIM = 64
V_DIM = 2 * HEAD_DIM
SUM_ROWS = 16
V_AUG = V_DIM + SUM_ROWS
ATT_HEADS = D_ATT // V_DIM
D_MIX_IN = 3 * D_CONV + 3 * D_ATT
D_FF = 2816
NORM_EPS = 1e-6
SUBLN_EPS = 1e-5
QK_SCALE = HEAD_DIM ** -0.5
LOG2_E = math.log2(math.e)

ROW_TILE = 1024
Q_TILE = 512
K_TILE = 512
FFN_ROW_TILE = 1024
FFN_HALO = 16
CONV_HALO = 8
FFN_CHUNK = 256
POS_SPLIT = 64
UNDERFLOW_LOG2 = 152.0
VMEM_LIMIT = 56 * 1024 * 1024


def _rms_norm(x, g, eps):
    return x * lax.rsqrt(jnp.mean(x * x, axis=-1, keepdims=True) + eps) * g


def _in_proj_kernel(x_ref, g_ref, coef_ref, w_ref, b_ref, ch_ref, ka_ref, kb_ref, qa_ref, qb_ref,
                    vt_ref, *, tiles_per_seq):
    h = _rms_norm(x_ref[...], g_ref[...], NORM_EPS).astype(BF16)

    def proj(col):
        return jnp.dot(h, w_ref[:, col * D_CONV:(col + 1) * D_CONV], preferred_element_type=F32)

    b_ref[...] = proj(0)
    ch_ref[...] = proj(1) * proj(2)

    pos = ((pl.program_id(0) % tiles_per_seq) * ROW_TILE
           + lax.broadcasted_iota(jnp.int32, (ROW_TILE, 1), 0))
    hi = (pos // POS_SPLIT).astype(F32)
    lo = (pos % POS_SPLIT).astype(F32)
    coef = coef_ref[...]
    k_pos = hi * coef[0:1] + lo * coef[1:2] + coef[2:3]
    q_pos = hi * coef[3:4] + lo * coef[4:5] + coef[5:6]
    lane = lax.broadcasted_iota(jnp.int32, (ROW_TILE, D_ATT), 1)
    first_map = (lane & HEAD_DIM) == 0
    q = proj(3) * (QK_SCALE * LOG2_E)
    qa_ref[...] = jnp.where(first_map, q, q_pos).T.astype(BF16)
    qb_ref[...] = jnp.where(first_map, q_pos, q).T.astype(BF16)
    k = proj(4)
    ka_ref[...] = jnp.where(first_map, k, k_pos).astype(BF16)
    kb_ref[...] = jnp.where(first_map, k_pos, k).astype(BF16)
    vt = proj(5).T.astype(BF16)
    ones_row = (lax.broadcasted_iota(jnp.int32, (SUM_ROWS, ROW_TILE), 0) == 0).astype(BF16)
    for hd in range(ATT_HEADS):
        vt_ref[hd * V_AUG:hd * V_AUG + V_DIM, :] = vt[hd * V_DIM:(hd + 1) * V_DIM]
        vt_ref[hd * V_AUG + V_DIM:(hd + 1) * V_AUG, :] = ones_row


def _position_coefficients():
    lane = jnp.arange(D_ATT)
    e = lane % HEAD_DIM
    slope = 2.0 ** (-8.0 * ((lane // V_DIM).astype(F32) + 1.0) / ATT_HEADS)
    pieces = []
    rest = np.float64(LOG2_E)
    for _ in range(3):
        piece = np.float64(rest.astype(BF16))
        pieces.append(piece)
        rest = rest - piece
    log2e_piece = jnp.asarray(pieces, F32)[e % 3]
    zero = jnp.zeros((D_ATT,), F32)
    rows = [
        jnp.where(e < 3, slope * POS_SPLIT, 0.0),
        jnp.where((e >= 3) & (e < 6), slope, 0.0),
        jnp.where((e >= 6) & (e < 12), log2e_piece, 0.0),
        jnp.where((e >= 6) & (e < 9), -slope * POS_SPLIT, 0.0),
        jnp.where((e >= 9) & (e < 12), -slope, 0.0),
        jnp.where(e < 6, log2e_piece, 0.0),
        zero, zero,
    ]
    return jnp.stack(rows).astype(F32)


def _in_proj(x2, g, coef, w, *, seq):
    t = x2.shape[0]
    grid = (t // ROW_TILE,)
    row = lambda i: (i, 0)
    col = lambda i: (0, i)
    const = lambda i: (0, 0)
    return pl.pallas_call(
        functools.partial(_in_proj_kernel, tiles_per_seq=seq // ROW_TILE),
        grid=grid,
        in_specs=[
            pl.BlockSpec((ROW_TILE, D_MODEL), row),
            pl.BlockSpec((1, D_MODEL), const),
            pl.BlockSpec((8, D_ATT), const),
            pl.BlockSpec((D_MODEL, D_MIX_IN), const, pipeline_mode=pl.Buffered(1)),
        ],
        out_specs=[
            pl.BlockSpec((ROW_TILE, D_CONV), row),
            pl.BlockSpec((ROW_TILE, D_CONV), row),
            pl.BlockSpec((ROW_TILE, D_ATT), row),
            pl.BlockSpec((ROW_TILE, D_ATT), row),
            pl.BlockSpec((D_ATT, ROW_TILE), col),
            pl.BlockSpec((D_ATT, ROW_TILE), col),
            pl.BlockSpec((ATT_HEADS * V_AUG, ROW_TILE), col),
        ],
        out_shape=[
            jax.ShapeDtypeStruct((t, D_CONV), F32),
            jax.ShapeDtypeStruct((t, D_CONV), F32),
            jax.ShapeDtypeStruct((t, D_ATT), BF16),
            jax.ShapeDtypeStruct((t, D_ATT), BF16),
            jax.ShapeDtypeStruct((D_ATT, t), BF16),
            jax.ShapeDtypeStruct((D_ATT, t), BF16),
            jax.ShapeDtypeStruct((ATT_HEADS * V_AUG, t), BF16),
        ],
        compiler_params=pltpu.CompilerParams(
            dimension_semantics=("arbitrary",), vmem_limit_bytes=VMEM_LIMIT),
        name="in_proj",
    )(x2, g, coef, w)


def _attn_kernel(slopes_ref, lam_ref, qa_ref, qb_ref, ka_ref, kb_ref, vt_ref, g_ref, o_ref,
                 m_ref, acc_ref, s0_ref, s1_ref, smax0_ref, smax1_ref, corr_ref, reach_ref,
                 *, seq, lambda_init):
    head = pl.program_id(1)
    nk = seq // K_TILE
    decay = LOG2_E * slopes_ref[head]

    reach_ref[0] = nk - 1

    @pl.when(decay * (K_TILE * (nk - 2) + 1) > UNDERFLOW_LOG2)
    def _():
        def max_norm2(ref, feature_axis, first_half):
            x = ref[...].astype(F32)
            idx = lax.broadcasted_iota(jnp.int32, x.shape, feature_axis)
            feature = (idx < HEAD_DIM) if first_half else (idx >= HEAD_DIM)
            n2 = jnp.sum(jnp.where(feature, x * x, 0.0), axis=feature_axis, keepdims=True)
            return jnp.max(n2, axis=1 - feature_axis, keepdims=True)

        q2 = jnp.maximum(max_norm2(qa_ref, 0, True), max_norm2(qb_ref, 0, False))
        k2 = jnp.maximum(max_norm2(ka_ref, 1, True), max_norm2(kb_ref, 1, False))
        budget = 2.02 * jnp.sqrt(q2 * k2) + UNDERFLOW_LOG2
        reach = jnp.floor((budget / decay - 1.0) * (1.0 / K_TILE)) + 1.0
        reach = jnp.where(reach < nk - 1, jnp.maximum(reach, 0.0), nk - 1.0)
        reach_ref[0] = reach.astype(jnp.int32)[0, 0]

    reach = reach_ref[0]

    key_minus_query = (lax.broadcasted_iota(jnp.int32, (K_TILE, Q_TILE), 0)
                       - lax.broadcasted_iota(jnp.int32, (K_TILE, Q_TILE), 1))
    corr_ref[...] = (-2.0 * decay) * jnp.maximum(key_minus_query, 0).astype(F32)

    lv = lam_ref[...]
    lam = (jnp.exp(jnp.sum(lv[0:1] * lv[1:2], axis=-1, keepdims=True))
           - jnp.exp(jnp.sum(lv[2:3] * lv[3:4], axis=-1, keepdims=True)) + lambda_init)
    buf0 = (s0_ref, smax0_ref)
    buf1 = (s1_ref, smax1_ref)

    def store_scores(s, buf, mp):
        s_ref, smax_ref = buf
        s_ref[mp] = s
        smax_ref[mp] = jnp.max(s, axis=0, keepdims=True)

    def diagonal_scores(qi, buf):
        t0 = pl.multiple_of(qi * Q_TILE, Q_TILE)
        for mp, (k_ref, q_ref) in enumerate(((ka_ref, qa_ref), (kb_ref, qb_ref))):
            s = jnp.dot(k_ref[pl.ds(t0, K_TILE), :], q_ref[:, pl.ds(t0, Q_TILE)],
                        preferred_element_type=F32)
            store_scores(s + corr_ref[...], buf, mp)

    diagonal_scores(0, buf0)

    def query_tile(qi, carry):
        q0 = pl.multiple_of(qi * Q_TILE, Q_TILE)
        row = lax.broadcasted_iota(jnp.int32, (V_DIM, Q_TILE), 0)
        qa_before = qa_ref[:, pl.ds(q0, Q_TILE)]
        qb_before = qb_ref[:, pl.ds(q0, Q_TILE)]
        qa_after = jnp.where(row >= HEAD_DIM, -qa_before, qa_before)
        qb_after = jnp.where(row < HEAD_DIM, -qb_before, qb_before)

        m_ref[...] = jnp.full(m_ref.shape, -1e30, F32)
        acc_ref[...] = jnp.zeros(acc_ref.shape, F32)

        lo = jnp.maximum(qi - reach, 0)
        hi = jnp.minimum(qi + reach, nk - 1)
        odd = (hi - lo + 1) % 2
        grow_hi = jnp.where(hi < nk - 1, odd, 0)
        hi = hi + grow_hi
        lo = lo - (odd - grow_hi)
        visits = hi - lo + 1

        def tile_of(n):
            j = lo + n - 1
            return jnp.where(n == 0, qi, jnp.where(j < qi, j, j + 1))

        def scores(n, buf):
            j = tile_of(n)
            k0 = pl.multiple_of(j * K_TILE, K_TILE)
            keys_before = j < qi
            for mp, (k_ref, q_before, q_after) in enumerate(
                    ((ka_ref, qa_before, qa_after), (kb_ref, qb_before, qb_after))):
                qt = jnp.where(keys_before, q_before, q_after)
                s = jnp.dot(k_ref[pl.ds(k0, K_TILE), :], qt, preferred_element_type=F32)
                store_scores(s, buf, mp)

        def absorb(n, buf):
            s_ref, smax_ref = buf
            k0 = pl.multiple_of(tile_of(n) * K_TILE, K_TILE)
            vt = vt_ref[:, pl.ds(k0, K_TILE)]
            for mp in range(2):
                s = s_ref[mp]
                m_old = m_ref[mp]
                m_new = jnp.maximum(m_old, smax_ref[mp])
                alpha = jnp.exp2(m_old - m_new)
                p = jnp.exp2(s - m_new)
                acc_ref[mp] = alpha * acc_ref[mp] + jnp.dot(vt, p.astype(BF16),
                                                            preferred_element_type=F32)
                m_ref[mp] = m_new

        def pair(p, c):
            n = 2 * p
            scores(n + 1, buf1)
            absorb(n, buf0)
            scores(n + 2, buf0)
            absorb(n + 1, buf1)
            return c

        lax.fori_loop(0, (visits - 2) // 2, pair, 0)
        scores(visits - 1, buf1)
        absorb(visits - 2, buf0)
        diagonal_scores(jnp.minimum(qi + 1, seq // Q_TILE - 1), buf0)
        absorb(visits - 1, buf1)

        def normalised(mp):
            a = acc_ref[mp]
            return a[:V_DIM] / a[V_DIM:V_DIM + 1]

        o = normalised(0) - lam * normalised(1)
        y = o * lax.rsqrt(jnp.mean(o * o, axis=0, keepdims=True) + SUBLN_EPS) * g_ref[...]
        o_ref[pl.ds(q0, Q_TILE), :] = (y * (1.0 - lambda_init)).T.astype(BF16)
        return carry

    lax.fori_loop(0, seq // Q_TILE, query_tile, 0)


def _attention(slopes, lam, qa, qb, ka, kb, vt, g_col, *, batch, seq, lambda_init):
    assert Q_TILE == K_TILE and (seq // K_TILE) % 2 == 0
    assert 8 % ATT_HEADS == 0
    t = batch * seq
    rows = lambda b, h: (b, h)
    cols = lambda b, h: (h, b)
    const = lambda b, h: (0, 0)
    return pl.pallas_call(
        functools.partial(_attn_kernel, seq=seq, lambda_init=lambda_init),
        grid=(batch, ATT_HEADS),
        in_specs=[
            pl.BlockSpec(memory_space=pltpu.SMEM),
            pl.BlockSpec((4, HEAD_DIM), const),
            pl.BlockSpec((V_DIM, seq), cols),
            pl.BlockSpec((V_DIM, seq), cols),
            pl.BlockSpec((seq, V_DIM), rows),
            pl.BlockSpec((seq, V_DIM), rows),
            pl.BlockSpec((V_AUG, seq), cols),
            pl.BlockSpec((V_DIM, 1), const),
        ],
        out_specs=pl.BlockSpec((seq, V_DIM), rows),
        out_shape=jax.ShapeDtypeStruct((t, D_ATT), BF16),
        scratch_shapes=[
            pltpu.VMEM((2, 1, Q_TILE), F32),
            pltpu.VMEM((2, V_AUG, Q_TILE), F32),
            pltpu.VMEM((2, K_TILE, Q_TILE), F32),
            pltpu.VMEM((2, K_TILE, Q_TILE), F32),
            pltpu.VMEM((2, 1, Q_TILE), F32),
            pltpu.VMEM((2, 1, Q_TILE), F32),
            pltpu.VMEM((K_TILE, Q_TILE), F32),
            pltpu.SMEM((1,), jnp.int32),
        ],
        compiler_params=pltpu.CompilerParams(
            dimension_semantics=("arbitrary", "arbitrary"), vmem_limit_bytes=VMEM_LIMIT),
        name="diff_attention",
    )(slopes, lam, qa, qb, ka, kb, vt, g_col)


def _out_proj_kernel(b_ref, ch_ref, chp_ref, chn_ref, ya_ref, cw_ref, gc_ref, gmat_ref, w_ref,
                     x_ref, gp_ref, o_ref, *, tiles_per_seq):
    i = pl.program_id(0)
    has_prev = (i % tiles_per_seq != 0).astype(F32)
    has_next = (i % tiles_per_seq != tiles_per_seq - 1).astype(F32)
    ch = ch_ref[...]
    rows = lax.broadcasted_iota(jnp.int32, ch.shape, 0)
    prev = jnp.where(rows == 0, chp_ref[CONV_HALO - 1:CONV_HALO, :] * has_prev,
                     pltpu.roll(ch, 1, 0))
    nxt = jnp.where(rows == ROW_TILE - 1, chn_ref[0:1, :] * has_next,
                    pltpu.roll(ch, ROW_TILE - 1, 0))
    cw = cw_ref[...]
    y = b_ref[...] * (prev * cw[0:1] + ch * cw[1:2] + nxt * cw[2:3])
    z = y * y
    z_hi = z.astype(BF16)
    z_lo = (z - z_hi.astype(F32)).astype(BF16)
    gmat = gmat_ref[...]
    ms = (jnp.dot(z_hi, gmat, preferred_element_type=F32)
          + jnp.dot(z_lo, gmat, preferred_element_type=F32))
    yc = (y * lax.rsqrt(ms + NORM_EPS) * gc_ref[...]).astype(BF16)
    mix = (jnp.dot(yc, w_ref[0:D_CONV, :], preferred_element_type=F32)
           + jnp.dot(ya_ref[...], w_ref[D_CONV:, :], preferred_element_type=F32))
    o_ref[...] = x_ref[...] + _rms_norm(mix, gp_ref[...], NORM_EPS)


def _out_proj(b, ch, ya, cw, gc, gmat, w, x2, gp, *, seq):
    t = x2.shape[0]
    n = t // ROW_TILE
    hb = ROW_TILE // CONV_HALO
    n_halo = t // CONV_HALO
    row = lambda i: (i, 0)
    const = lambda i: (0, 0)
    return pl.pallas_call(
        functools.partial(_out_proj_kernel, tiles_per_seq=seq // ROW_TILE),
        grid=(n,),
        in_specs=[
            pl.BlockSpec((ROW_TILE, D_CONV), row),
            pl.BlockSpec((ROW_TILE, D_CONV), row),
            pl.BlockSpec((CONV_HALO, D_CONV), lambda i: (jnp.maximum(i * hb - 1, 0), 0)),
            pl.BlockSpec((CONV_HALO, D_CONV), lambda i: (jnp.minimum((i + 1) * hb, n_halo - 1), 0)),
            pl.BlockSpec((ROW_TILE, D_ATT), row),
            pl.BlockSpec((3, D_CONV), const),
            pl.BlockSpec((1, D_CONV), const),
            pl.BlockSpec((D_CONV, D_CONV), const),
            pl.BlockSpec((D_MODEL, D_MODEL), const, pipeline_mode=pl.Buffered(1)),
            pl.BlockSpec((ROW_TILE, D_MODEL), row),
            pl.BlockSpec((1, D_MODEL), const),
        ],
        out_specs=pl.BlockSpec((ROW_TILE, D_MODEL), row),
        out_shape=jax.ShapeDtypeStruct((t, D_MODEL), F32),
        compiler_params=pltpu.CompilerParams(
            dimension_semantics=("arbitrary",), vmem_limit_bytes=VMEM_LIMIT),
        name="out_proj",
    )(b, ch, ch, ch, ya, cw, gc, gmat, w, x2, gp)


def _ffn_kernel(x_ref, xp_ref, xn_ref, g_ref, wu_ref, cw_ref, cb_ref, wd_ref, gp_ref, o_ref,
                hs_ref, act_ref, *, tiles_per_seq):
    i = pl.program_id(0)
    has_prev = (i % tiles_per_seq != 0).astype(F32)
    has_next = (i % tiles_per_seq != tiles_per_seq - 1).astype(F32)
    g = g_ref[...]
    x = x_ref[...]
    halo_row = lax.broadcasted_iota(jnp.int32, (FFN_HALO, D_MODEL), 0)
    h_prev = _rms_norm(xp_ref[...], g, NORM_EPS) * has_prev
    h_next = _rms_norm(xn_ref[...], g, NORM_EPS) * has_next
    hs_ref[0:FFN_HALO, :] = jnp.where(halo_row == FFN_HALO - 1, h_prev,
                                      jnp.where(halo_row == 0, h_next, 0.0)).astype(BF16)
    hs_ref[FFN_HALO:, :] = _rms_norm(x, g, NORM_EPS).astype(BF16)
    hs = hs_ref[...]
    ext = FFN_ROW_TILE + FFN_HALO

    def conv_chunk(c0):
        cols = slice(c0, c0 + FFN_CHUNK)
        u = jnp.dot(hs, wu_ref[:, cols], preferred_element_type=F32)
        cw = cw_ref[:, cols]
        mid = slice(FFN_HALO, FFN_HALO + FFN_ROW_TILE)
        return (pltpu.roll(u, 1, 0)[mid] * cw[0:1] + u[mid] * cw[1:2]
                + pltpu.roll(u, ext - 1, 0)[mid] * cw[2:3] + cb_ref[:, cols])

    for c in range(D_FF // FFN_CHUNK):
        gate = conv_chunk(c * FFN_CHUNK)
        up = conv_chunk(D_FF + c * FFN_CHUNK)
        act = gate * (1.0 / (1.0 + jnp.exp(-gate))) * up
        act_ref[:, c * FFN_CHUNK:(c + 1) * FFN_CHUNK] = act.astype(BF16)

    f = jnp.dot(act_ref[...], wd_ref[...], preferred_element_type=F32)
    o_ref[...] = x + _rms_norm(f, gp_ref[...], NORM_EPS)


def _ffn(x1, g, wu, cw, cb, wd, gp, *, seq):
    t = x1.shape[0]
    n = t // FFN_ROW_TILE
    hb = FFN_ROW_TILE // FFN_HALO
    n_halo = t // FFN_HALO
    row = lambda i: (i, 0)
    const = lambda i: (0, 0)
    return pl.pallas_call(
        functools.partial(_ffn_kernel, tiles_per_seq=seq // FFN_ROW_TILE),
        grid=(n,),
        in_specs=[
            pl.BlockSpec((FFN_ROW_TILE, D_MODEL), row),
            pl.BlockSpec((FFN_HALO, D_MODEL), lambda i: (jnp.maximum(i * hb - 1, 0), 0)),
            pl.BlockSpec((FFN_HALO, D_MODEL), lambda i: (jnp.minimum((i + 1) * hb, n_halo - 1), 0)),
            pl.BlockSpec((1, D_MODEL), const),
            pl.BlockSpec((D_MODEL, 2 * D_FF), const, pipeline_mode=pl.Buffered(1)),
            pl.BlockSpec((3, 2 * D_FF), const),
            pl.BlockSpec((1, 2 * D_FF), const),
            pl.BlockSpec((D_FF, D_MODEL), const, pipeline_mode=pl.Buffered(1)),
            pl.BlockSpec((1, D_MODEL), const),
        ],
        out_specs=pl.BlockSpec((FFN_ROW_TILE, D_MODEL), row),
        out_shape=jax.ShapeDtypeStruct((t, D_MODEL), F32),
        scratch_shapes=[
            pltpu.VMEM((FFN_ROW_TILE + FFN_HALO, D_MODEL), BF16),
            pltpu.VMEM((FFN_ROW_TILE, D_FF), BF16),
        ],
        compiler_params=pltpu.CompilerParams(
            dimension_semantics=("arbitrary",), vmem_limit_bytes=VMEM_LIMIT),
        name="conv_glu_ffn",
    )(x1, x1, x1, g, wu, cw, cb, wd, gp)


def kernel(x, g_mix_pre, w_mix_in, conv_w, g_conv_out, lambda_q1, lambda_k1, lambda_q2, lambda_k2,
           g_subln, w_mix_out, g_mix_post, g_ffn_pre, w_ffn_up, ffn_conv_w, ffn_conv_b, w_ffn_down,
           g_ffn_post):
    batch, seq, d = x.shape
    depth = g_mix_pre.shape[0]
    assert d == D_MODEL and seq % ROW_TILE == 0 and seq % Q_TILE == 0 and seq % K_TILE == 0
    assert seq % FFN_ROW_TILE == 0
    assert seq <= POS_SPLIT * POS_SPLIT
    t = batch * seq
    x2 = x.reshape(t, d)
    slopes = 2.0 ** (-8.0 * (jnp.arange(ATT_HEADS, dtype=F32) + 1.0) / ATT_HEADS)
    coef = _position_coefficients()
    group = jnp.arange(D_CONV) // CONV_GROUP_DIM
    gmat = jnp.where(group[:, None] == group[None, :], 1.0 / CONV_GROUP_DIM, 0.0).astype(BF16)

    for layer in range(depth):
        lambda_init = 0.8 - 0.6 * math.exp(-0.3 * layer)
        lam = jnp.stack([lambda_q1[layer], lambda_k1[layer], lambda_q2[layer], lambda_k2[layer]]).astype(F32)
        b, ch, ka, kb, qa, qb, vt = _in_proj(x2, g_mix_pre[layer][None], coef,
                                             w_mix_in[layer].astype(BF16), seq=seq)
        ya = _attention(slopes, lam, qa, qb, ka, kb, vt, g_subln[layer][:, None],
                        batch=batch, seq=seq, lambda_init=lambda_init)
        x2 = _out_proj(b, ch, ya, conv_w[layer], g_conv_out[layer][None], gmat,
                       w_mix_out[layer].astype(BF16), x2, g_mix_post[layer][None], seq=seq)
        x2 = _ffn(x2, g_ffn_pre[layer][None], w_ffn_up[layer].astype(BF16), ffn_conv_w[layer],
                  ffn_conv_b[layer][None], w_ffn_down[layer].astype(BF16), g_ffn_post[layer][None],
                  seq=seq)
    return x2.reshape(batch, seq, d)
```

```python
import functools
import math

import jax
import jax.numpy as jnp
import numpy as np
from jax import lax
from jax.experimental import pallas as pl
from jax.experimental.pallas import tpu as pltpu

F32 = jnp.float32
BF16 = jnp.bfloat16

D_MODEL = 1024
D_CONV = 512
D_ATT = 512
CONV_GROUP_DIM = 64
HEAD_DIM = 64
V_DIM = 2 * HEAD_DIM
SUM_ROWS = 16
V_AUG = V_DIM + SUM_ROWS
ATT_HEADS = D_ATT // V_DIM
D_MIX_IN = 3 * D_CONV + 3 * D_ATT
D_FF = 2816
NORM_EPS = 1e-6
SUBLN_EPS = 1e-5
QK_SCALE = HEAD_DIM ** -0.5
LOG2_E = math.log2(math.e)

ROW_TILE = 1024
Q_TILE = 512
K_TILE = 512
FFN_ROW_TILE = 1024
FFN_HALO = 16
CONV_HALO = 8
FFN_CHUNK = 256
POS_SPLIT = 64
UNDERFLOW_LOG2 = 152.0
VMEM_LIMIT = 56 * 1024 * 1024


def _rms_norm(x, g, eps):
    return x * lax.rsqrt(jnp.mean(x * x, axis=-1, keepdims=True) + eps) * g


def _in_proj_kernel(x_ref, g_ref, coef_ref, w_ref, b_ref, ch_ref, ka_ref, kb_ref, qa_ref, qb_ref,
                    vt_ref, *, tiles_per_seq):
    h = _rms_norm(x_ref[...], g_ref[...], NORM_EPS).astype(BF16)

    def proj(col):
        return jnp.dot(h, w_ref[:, col * D_CONV:(col + 1) * D_CONV], preferred_element_type=F32)

    b_ref[...] = proj(0)
    ch_ref[...] = proj(1) * proj(2)

    pos = ((pl.program_id(0) % tiles_per_seq) * ROW_TILE
           + lax.broadcasted_iota(jnp.int32, (ROW_TILE, 1), 0))
    hi = (pos // POS_SPLIT).astype(F32)
    lo = (pos % POS_SPLIT).astype(F32)
    coef = coef_ref[...]
    k_pos = hi * coef[0:1] + lo * coef[1:2] + coef[2:3]
    q_pos = hi * coef[3:4] + lo * coef[4:5] + coef[5:6]
    lane = lax.broadcasted_iota(jnp.int32, (ROW_TILE, D_ATT), 1)
    first_map = (lane & HEAD_DIM) == 0
    q = proj(3) * (QK_SCALE * LOG2_E)
    qa_ref[...] = jnp.where(first_map, q, q_pos).T.astype(BF16)
    qb_ref[...] = jnp.where(first_map, q_pos, q).T.astype(BF16)
    k = proj(4)
    ka_ref[...] = jnp.where(first_map, k, k_pos).astype(BF16)
    kb_ref[...] = jnp.where(first_map, k_pos, k).astype(BF16)
    vt = proj(5).T.astype(BF16)
    ones_row = (lax.broadcasted_iota(jnp.int32, (SUM_ROWS, ROW_TILE), 0) == 0).astype(BF16)
    for hd in range(ATT_HEADS):
        vt_ref[hd * V_AUG:hd * V_AUG + V_DIM, :] = vt[hd * V_DIM:(hd + 1) * V_DIM]
        vt_ref[hd * V_AUG + V_DIM:(hd + 1) * V_AUG, :] = ones_row


def _position_coefficients():
    lane = jnp.arange(D_ATT)
    e = lane % HEAD_DIM
    slope = 2.0 ** (-8.0 * ((lane // V_DIM).astype(F32) + 1.0) / ATT_HEADS)
    pieces = []
    rest = np.float64(LOG2_E)
    for _ in range(3):
        piece = np.float64(rest.astype(BF16))
        pieces.append(piece)
        rest = rest - piece
    log2e_piece = jnp.asarray(pieces, F32)[e % 3]
    zero = jnp.zeros((D_ATT,), F32)
    rows = [
        jnp.where(e < 3, slope * POS_SPLIT, 0.0),
        jnp.where((e >= 3) & (e < 6), slope, 0.0),
        jnp.where((e >= 6) & (e < 12), log2e_piece, 0.0),
        jnp.where((e >= 6) & (e < 9), -slope * POS_SPLIT, 0.0),
        jnp.where((e >= 9) & (e < 12), -slope, 0.0),
        jnp.where(e < 6, log2e_piece, 0.0),
        zero, zero,
    ]
    return jnp.stack(rows).astype(F32)


def _in_proj(x2, g, coef, w, *, seq):
    t = x2.shape[0]
    grid = (t // ROW_TILE,)
    row = lambda i: (i, 0)
    col = lambda i: (0, i)
    const = lambda i: (0, 0)
    return pl.pallas_call(
        functools.partial(_in_proj_kernel, tiles_per_seq=seq // ROW_TILE),
        grid=grid,
        in_specs=[
            pl.BlockSpec((ROW_TILE, D_MODEL), row),
            pl.BlockSpec((1, D_MODEL), const),
            pl.BlockSpec((8, D_ATT), const),
            pl.BlockSpec((D_MODEL, D_MIX_IN), const, pipeline_mode=pl.Buffered(1)),
        ],
        out_specs=[
            pl.BlockSpec((ROW_TILE, D_CONV), row),
            pl.BlockSpec((ROW_TILE, D_CONV), row),
            pl.BlockSpec((ROW_TILE, D_ATT), row),
            pl.BlockSpec((ROW_TILE, D_ATT), row),
            pl.BlockSpec((D_ATT, ROW_TILE), col),
            pl.BlockSpec((D_ATT, ROW_TILE), col),
            pl.BlockSpec((ATT_HEADS * V_AUG, ROW_TILE), col),
        ],
        out_shape=[
            jax.ShapeDtypeStruct((t, D_CONV), F32),
            jax.ShapeDtypeStruct((t, D_CONV), F32),
            jax.ShapeDtypeStruct((t, D_ATT), BF16),
            jax.ShapeDtypeStruct((t, D_ATT), BF16),
            jax.ShapeDtypeStruct((D_ATT, t), BF16),
            jax.ShapeDtypeStruct((D_ATT, t), BF16),
            jax.ShapeDtypeStruct((ATT_HEADS * V_AUG, t), BF16),
        ],
        compiler_params=pltpu.CompilerParams(
            dimension_semantics=("arbitrary",), vmem_limit_bytes=VMEM_LIMIT),
        name="in_proj",
    )(x2, g, coef, w)


def _attn_kernel(slopes_ref, lam_ref, qa_ref, qb_ref, ka_ref, kb_ref, vt_ref, g_ref, o_ref,
                 m_ref, acc_ref, s0_ref, s1_ref, smax0_ref, smax1_ref, corr_ref, reach_ref,
                 *, seq, lambda_init):
    head = pl.program_id(1)
    nk = seq // K_TILE
    decay = LOG2_E * slopes_ref[head]

    reach_ref[0] = nk - 1

    @pl.when(decay * (K_TILE * (nk - 2) + 1) > UNDERFLOW_LOG2)
    def _():
        def max_norm2(ref, feature_axis, first_half):
            x = ref[...].astype(F32)
            idx = lax.broadcasted_iota(jnp.int32, x.shape, feature_axis)
            feature = (idx < HEAD_DIM) if first_half else (idx >= HEAD_DIM)
            n2 = jnp.sum(jnp.where(feature, x * x, 0.0), axis=feature_axis, keepdims=True)
            return jnp.max(n2, axis=1 - feature_axis, keepdims=True)

        q2 = jnp.maximum(max_norm2(qa_ref, 0, True), max_norm2(qb_ref, 0, False))
        k2 = jnp.maximum(max_norm2(ka_ref, 1, True), max_norm2(kb_ref, 1, False))
        budget = 2.02 * jnp.sqrt(q2 * k2) + UNDERFLOW_LOG2
        reach = jnp.floor((budget / decay - 1.0) * (1.0 / K_TILE)) + 1.0
        reach = jnp.where(reach < nk - 1, jnp.maximum(reach, 0.0), nk - 1.0)
        reach_ref[0] = reach.astype(jnp.int32)[0, 0]

    reach = reach_ref[0]

    key_minus_query = (lax.broadcasted_iota(jnp.int32, (K_TILE, Q_TILE), 0)
                       - lax.broadcasted_iota(jnp.int32, (K_TILE, Q_TILE), 1))
    corr_ref[...] = (-2.0 * decay) * jnp.maximum(key_minus_query, 0).astype(F32)

    lv = lam_ref[...]
    lam = (jnp.exp(jnp.sum(lv[0:1] * lv[1:2], axis=-1, keepdims=True))
           - jnp.exp(jnp.sum(lv[2:3] * lv[3:4], axis=-1, keepdims=True)) + lambda_init)
    buf0 = (s0_ref, smax0_ref)
    buf1 = (s1_ref, smax1_ref)

    def store_scores(s, buf, mp):
        s_ref, smax_ref = buf
        s_ref[mp] = s
        smax_ref[mp] = jnp.max(s, axis=0, keepdims=True)

    def diagonal_scores(qi, buf):
        t0 = pl.multiple_of(qi * Q_TILE, Q_TILE)
        for mp, (k_ref, q_ref) in enumerate(((ka_ref, qa_ref), (kb_ref, qb_ref))):
            s = jnp.dot(k_ref[pl.ds(t0, K_TILE), :], q_ref[:, pl.ds(t0, Q_TILE)],
                        preferred_element_type=F32)
            store_scores(s + corr_ref[...], buf, mp)

    diagonal_scores(0, buf0)

    def query_tile(qi, carry):
        q0 = pl.multiple_of(qi * Q_TILE, Q_TILE)
        row = lax.broadcasted_iota(jnp.int32, (V_DIM, Q_TILE), 0)
        qa_before = qa_ref[:, pl.ds(q0, Q_TILE)]
        qb_before = qb_ref[:, pl.ds(q0, Q_TILE)]
        qa_after = jnp.where(row >= HEAD_DIM, -qa_before, qa_before)
        qb_after = jnp.where(row < HEAD_DIM, -qb_before, qb_before)

        m_ref[...] = jnp.full(m_ref.shape, -1e30, F32)
        acc_ref[...] = jnp.zeros(acc_ref.shape, F32)

        lo = jnp.maximum(qi - reach, 0)
        hi = jnp.minimum(qi + reach, nk - 1)
        odd = (hi - lo + 1) % 2
        grow_hi = jnp.where(hi < nk - 1, odd, 0)
        hi = hi + grow_hi
        lo = lo - (odd - grow_hi)
        visits = hi - lo + 1

        def tile_of(n):
            j = lo + n - 1
            return jnp.where(n == 0, qi, jnp.where(j < qi, j, j + 1))

        def scores(n, buf):
            j = tile_of(n)
            k0 = pl.multiple_of(j * K_TILE, K_TILE)
            keys_before = j < qi
            for mp, (k_ref, q_before, q_after) in enumerate(
                    ((ka_ref, qa_before, qa_after), (kb_ref, qb_before, qb_after))):
                qt = jnp.where(keys_before, q_before, q_after)
                s = jnp.dot(k_ref[pl.ds(k0, K_TILE), :], qt, preferred_element_type=F32)
                store_scores(s, buf, mp)

        def absorb(n, buf):
            s_ref, smax_ref = buf
            k0 = pl.multiple_of(tile_of(n) * K_TILE, K_TILE)
            vt = vt_ref[:, pl.ds(k0, K_TILE)]
            for mp in range(2):
                s = s_ref[mp]
                m_old = m_ref[mp]
                m_new = jnp.maximum(m_old, smax_ref[mp])
                alpha = jnp.exp2(m_old - m_new)
                p = jnp.exp2(s - m_new)
                acc_ref[mp] = alpha * acc_ref[mp] + jnp.dot(vt, p.astype(BF16),
                                                            preferred_element_type=F32)
                m_ref[mp] = m_new

        def pair(p, c):
            n = 2 * p
            scores(n + 1, buf1)
            absorb(n, buf0)
            scores(n + 2, buf0)
            absorb(n + 1, buf1)
            return c

        lax.fori_loop(0, (visits - 2) // 2, pair, 0)
        scores(visits - 1, buf1)
        absorb(visits - 2, buf0)
        diagonal_scores(jnp.minimum(qi + 1, seq // Q_TILE - 1), buf0)
        absorb(visits - 1, buf1)

        def normalised(mp):
            a = acc_ref[mp]
            return a[:V_DIM] / a[V_DIM:V_DIM + 1]

        o = normalised(0) - lam * normalised(1)
        y = o * lax.rsqrt(jnp.mean(o * o, axis=0, keepdims=True) + SUBLN_EPS) * g_ref[...]
        o_ref[pl.ds(q0, Q_TILE), :] = (y * (1.0 - lambda_init)).T.astype(BF16)
        return carry

    lax.fori_loop(0, seq // Q_TILE, query_tile, 0)


def _attention(slopes, lam, qa, qb, ka, kb, vt, g_col, *, batch, seq, lambda_init):
    assert Q_TILE == K_TILE and (seq // K_TILE) % 2 == 0
    assert 8 % ATT_HEADS == 0
    t = batch * seq
    rows = lambda b, h: (b, h)
    cols = lambda b, h: (h, b)
    const = lambda b, h: (0, 0)
    return pl.pallas_call(
        functools.partial(_attn_kernel, seq=seq, lambda_init=lambda_init),
        grid=(batch, ATT_HEADS),
        in_specs=[
            pl.BlockSpec(memory_space=pltpu.SMEM),
            pl.BlockSpec((4, HEAD_DIM), const),
            pl.BlockSpec((V_DIM, seq), cols),
            pl.BlockSpec((V_DIM, seq), cols),
            pl.BlockSpec((seq, V_DIM), rows),
            pl.BlockSpec((seq, V_DIM), rows),
            pl.BlockSpec((V_AUG, seq), cols),
            pl.BlockSpec((V_DIM, 1), const),
        ],
        out_specs=pl.BlockSpec((seq, V_DIM), rows),
        out_shape=jax.ShapeDtypeStruct((t, D_ATT), BF16),
        scratch_shapes=[
            pltpu.VMEM((2, 1, Q_TILE), F32),
            pltpu.VMEM((2, V_AUG, Q_TILE), F32),
            pltpu.VMEM((2, K_TILE, Q_TILE), F32),
            pltpu.VMEM((2, K_TILE, Q_TILE), F32),
            pltpu.VMEM((2, 1, Q_TILE), F32),
            pltpu.VMEM((2, 1, Q_TILE), F32),
            pltpu.VMEM((K_TILE, Q_TILE), F32),
            pltpu.SMEM((1,), jnp.int32),
        ],
        compiler_params=pltpu.CompilerParams(
            dimension_semantics=("arbitrary", "arbitrary"), vmem_limit_bytes=VMEM_LIMIT),
        name="diff_attention",
    )(slopes, lam, qa, qb, ka, kb, vt, g_col)


def _out_proj_kernel(b_ref, ch_ref, chp_ref, chn_ref, ya_ref, cw_ref, gc_ref, w_ref,
                     x_ref, gp_ref, o_ref, *, tiles_per_seq):
    i = pl.program_id(0)
    has_prev = (i % tiles_per_seq != 0).astype(F32)
    has_next = (i % tiles_per_seq != tiles_per_seq - 1).astype(F32)
    ch = ch_ref[...]
    rows = lax.broadcasted_iota(jnp.int32, ch.shape, 0)
    prev = jnp.where(rows == 0, chp_ref[CONV_HALO - 1:CONV_HALO, :] * has_prev,
                     pltpu.roll(ch, 1, 0))
    nxt = jnp.where(rows == ROW_TILE - 1, chn_ref[0:1, :] * has_next,
                    pltpu.roll(ch, ROW_TILE - 1, 0))
    cw = cw_ref[...]
    y = b_ref[...] * (prev * cw[0:1] + ch * cw[1:2] + nxt * cw[2:3])
    z = y * y
    first_group = lax.broadcasted_iota(jnp.int32, (ROW_TILE, 2 * CONV_GROUP_DIM), 1) < CONV_GROUP_DIM
    ms_blocks = []
    for blk in range(D_CONV // (2 * CONV_GROUP_DIM)):
        zb = z[:, blk * 2 * CONV_GROUP_DIM:(blk + 1) * 2 * CONV_GROUP_DIM]
        s_first = jnp.sum(jnp.where(first_group, zb, 0.0), axis=1, keepdims=True)
        s_second = jnp.sum(jnp.where(first_group, 0.0, zb), axis=1, keepdims=True)
        ms_blocks.append(jnp.where(first_group, s_first, s_second))
    ms = jnp.concatenate(ms_blocks, axis=1) * (1.0 / CONV_GROUP_DIM)
    yc = (y * lax.rsqrt(ms + NORM_EPS) * gc_ref[...]).astype(BF16)
    mix = (jnp.dot(yc, w_ref[0:D_CONV, :], preferred_element_type=F32)
           + jnp.dot(ya_ref[...], w_ref[D_CONV:, :], preferred_element_type=F32))
    o_ref[...] = x_ref[...] + _rms_norm(mix, gp_ref[...], NORM_EPS)


def _out_proj(b, ch, ya, cw, gc, w, x2, gp, *, seq):
    t = x2.shape[0]
    n = t // ROW_TILE
    hb = ROW_TILE // CONV_HALO
    n_halo = t // CONV_HALO
    row = lambda i: (i, 0)
    const = lambda i: (0, 0)
    return pl.pallas_call(
        functools.partial(_out_proj_kernel, tiles_per_seq=seq // ROW_TILE),
        grid=(n,),
        in_specs=[
            pl.BlockSpec((ROW_TILE, D_CONV), row),
            pl.BlockSpec((ROW_TILE, D_CONV), row),
            pl.BlockSpec((CONV_HALO, D_CONV), lambda i: (jnp.maximum(i * hb - 1, 0), 0)),
            pl.BlockSpec((CONV_HALO, D_CONV), lambda i: (jnp.minimum((i + 1) * hb, n_halo - 1), 0)),
            pl.BlockSpec((ROW_TILE, D_ATT), row),
            pl.BlockSpec((3, D_CONV), const),
            pl.BlockSpec((1, D_CONV), const),
            pl.BlockSpec((D_MODEL, D_MODEL), const, pipeline_mode=pl.Buffered(1)),
            pl.BlockSpec((ROW_TILE, D_MODEL), row),
            pl.BlockSpec((1, D_MODEL), const),
        ],
        out_specs=pl.BlockSpec((ROW_TILE, D_MODEL), row),
        out_shape=jax.ShapeDtypeStruct((t, D_MODEL), F32),
        compiler_params=pltpu.CompilerParams(
            dimension_semantics=("arbitrary",), vmem_limit_bytes=VMEM_LIMIT),
        name="out_proj",
    )(b, ch, ch, ch, ya, cw, gc, w, x2, gp)


def _ffn_kernel(x_ref, xp_ref, xn_ref, g_ref, wu_ref, cw_ref, cb_ref, wd_ref, gp_ref, o_ref,
                hs_ref, act_ref, *, tiles_per_seq):
    i = pl.program_id(0)
    has_prev = (i % tiles_per_seq != 0).astype(F32)
    has_next = (i % tiles_per_seq != tiles_per_seq - 1).astype(F32)
    g = g_ref[...]
    x = x_ref[...]
    halo_row = lax.broadcasted_iota(jnp.int32, (FFN_HALO, D_MODEL), 0)
    h_prev = _rms_norm(xp_ref[...], g, NORM_EPS) * has_prev
    h_next = _rms_norm(xn_ref[...], g, NORM_EPS) * has_next
    hs_ref[0:FFN_HALO, :] = jnp.where(halo_row == FFN_HALO - 1, h_prev,
                                      jnp.where(halo_row == 0, h_next, 0.0)).astype(BF16)
    hs_ref[FFN_HALO:, :] = _rms_norm(x, g, NORM_EPS).astype(BF16)
    hs = hs_ref[...]
    ext = FFN_ROW_TILE + FFN_HALO

    def conv_chunk(c0):
        cols = slice(c0, c0 + FFN_CHUNK)
        u = jnp.dot(hs, wu_ref[:, cols], preferred_element_type=F32)
        cw = cw_ref[:, cols]
        mid = slice(FFN_HALO, FFN_HALO + FFN_ROW_TILE)
        return (pltpu.roll(u, 1, 0)[mid] * cw[0:1] + u[mid] * cw[1:2]
                + pltpu.roll(u, ext - 1, 0)[mid] * cw[2:3] + cb_ref[:, cols])

    for c in range(D_FF // FFN_CHUNK):
        gate = conv_chunk(c * FFN_CHUNK)
        up = conv_chunk(D_FF + c * FFN_CHUNK)
        act = gate * (1.0 / (1.0 + jnp.exp(-gate))) * up
        act_ref[:, c * FFN_CHUNK:(c + 1) * FFN_CHUNK] = act.astype(BF16)

    f = jnp.dot(act_ref[...], wd_ref[...], preferred_element_type=F32)
    o_ref[...] = x + _rms_norm(f, gp_ref[...], NORM_EPS)


def _ffn(x1, g, wu, cw, cb, wd, gp, *, seq):
    t = x1.shape[0]
    n = t // FFN_ROW_TILE
    hb = FFN_ROW_TILE // FFN_HALO
    n_halo = t // FFN_HALO
    row = lambda i: (i, 0)
    const = lambda i: (0, 0)
    return pl.pallas_call(
        functools.partial(_ffn_kernel, tiles_per_seq=seq // FFN_ROW_TILE),
        grid=(n,),
        in_specs=[
            pl.BlockSpec((FFN_ROW_TILE, D_MODEL), row),
            pl.BlockSpec((FFN_HALO, D_MODEL), lambda i: (jnp.maximum(i * hb - 1, 0), 0)),
            pl.BlockSpec((FFN_HALO, D_MODEL), lambda i: (jnp.minimum((i + 1) * hb, n_halo - 1), 0)),
            pl.BlockSpec((1, D_MODEL), const),
            pl.BlockSpec((D_MODEL, 2 * D_FF), const, pipeline_mode=pl.Buffered(1)),
            pl.BlockSpec((3, 2 * D_FF), const),
            pl.BlockSpec((1, 2 * D_FF), const),
            pl.BlockSpec((D_FF, D_MODEL), const, pipeline_mode=pl.Buffered(1)),
            pl.BlockSpec((1, D_MODEL), const),
        ],
        out_specs=pl.BlockSpec((FFN_ROW_TILE, D_MODEL), row),
        out_shape=jax.ShapeDtypeStruct((t, D_MODEL), F32),
        scratch_shapes=[
            pltpu.VMEM((FFN_ROW_TILE + FFN_HALO, D_MODEL), BF16),
            pltpu.VMEM((FFN_ROW_TILE, D_FF), BF16),
        ],
        compiler_params=pltpu.CompilerParams(
            dimension_semantics=("arbitrary",), vmem_limit_bytes=VMEM_LIMIT),
        name="conv_glu_ffn",
    )(x1, x1, x1, g, wu, cw, cb, wd, gp)


def kernel(x, g_mix_pre, w_mix_in, conv_w, g_conv_out, lambda_q1, lambda_k1, lambda_q2, lambda_k2,
           g_subln, w_mix_out, g_mix_post, g_ffn_pre, w_ffn_up, ffn_conv_w, ffn_conv_b, w_ffn_down,
           g_ffn_post):
    batch, seq, d = x.shape
    depth = g_mix_pre.shape[0]
    assert d == D_MODEL and seq % ROW_TILE == 0 and seq % Q_TILE == 0 and seq % K_TILE == 0
    assert seq % FFN_ROW_TILE == 0
    assert seq <= POS_SPLIT * POS_SPLIT
    t = batch * seq
    x2 = x.reshape(t, d)
    slopes = 2.0 ** (-8.0 * (jnp.arange(ATT_HEADS, dtype=F32) + 1.0) / ATT_HEADS)
    coef = _position_coefficients()

    for layer in range(depth):
        lambda_init = 0.8 - 0.6 * math.exp(-0.3 * layer)
        lam = jnp.stack([lambda_q1[layer], lambda_k1[layer], lambda_q2[layer], lambda_k2[layer]]).astype(F32)
        b, ch, ka, kb, qa, qb, vt = _in_proj(x2, g_mix_pre[layer][None], coef,
                                             w_mix_in[layer].astype(BF16), seq=seq)
        ya = _attention(slopes, lam, qa, qb, ka, kb, vt, g_subln[layer][:, None],
                        batch=batch, seq=seq, lambda_init=lambda_init)
        x2 = _out_proj(b, ch, ya, conv_w[layer], g_conv_out[layer][None],
                       w_mix_out[layer].astype(BF16), x2, g_mix_post[layer][None], seq=seq)
        x2 = _ffn(x2, g_ffn_pre[layer][None], w_ffn_up[layer].astype(BF16), ffn_conv_w[layer],
                  ffn_conv_b[layer][None], w_ffn_down[layer].astype(BF16), g_ffn_post[layer][None],
                  seq=seq)
    return x2.reshape(batch, seq, d)
```

```python
import functools
import math

import jax
import jax.numpy as jnp
import numpy as np
from jax import lax
from jax.experimental import pallas as pl
from jax.experimental.pallas import tpu as pltpu

F32 = jnp.float32
BF16 = jnp.bfloat16

D_MODEL = 1024
D_CONV = 512
D_ATT = 512
CONV_GROUP_DIM = 64
HEAD_DIM = 64
V_DIM = 2 * HEAD_DIM
SUM_ROWS = 16
V_AUG = V_DIM + SUM_ROWS
ATT_HEADS = D_ATT // V_DIM
D_MIX_IN = 3 * D_CONV + 3 * D_ATT
D_FF = 2816
NORM_EPS = 1e-6
SUBLN_EPS = 1e-5
QK_SCALE = HEAD_DIM ** -0.5
LOG2_E = math.log2(math.e)

ROW_TILE = 1024
Q_TILE = 512
K_TILE = 512
FFN_ROW_TILE = 1024
FFN_HALO = 16
CONV_HALO = 8
FFN_CHUNK = 256
POS_SPLIT = 64
UNDERFLOW_LOG2 = 152.0
VMEM_LIMIT = 56 * 1024 * 1024


def _rms_norm(x, g, eps):
    return x * lax.rsqrt(jnp.mean(x * x, axis=-1, keepdims=True) + eps) * g


def _in_proj_kernel(x_ref, g_ref, coef_ref, w_ref, b_ref, ch_ref, ka_ref, kb_ref, qa_ref, qb_ref,
                    vt_ref, *, tiles_per_seq):
    h = _rms_norm(x_ref[...], g_ref[...], NORM_EPS).astype(BF16)

    def proj(col):
        return jnp.dot(h, w_ref[:, col * D_CONV:(col + 1) * D_CONV], preferred_element_type=F32)

    b_ref[...] = proj(0)
    ch_ref[...] = proj(1) * proj(2)

    pos = ((pl.program_id(0) % tiles_per_seq) * ROW_TILE
           + lax.broadcasted_iota(jnp.int32, (ROW_TILE, 1), 0))
    hi = (pos // POS_SPLIT).astype(F32)
    lo = (pos % POS_SPLIT).astype(F32)
    coef = coef_ref[...]
    k_pos = hi * coef[0:1] + lo * coef[1:2] + coef[2:3]
    q_pos = hi * coef[3:4] + lo * coef[4:5] + coef[5:6]
    lane = lax.broadcasted_iota(jnp.int32, (ROW_TILE, D_ATT), 1)
    first_map = (lane & HEAD_DIM) == 0
    q = proj(3) * (QK_SCALE * LOG2_E)
    qa_ref[...] = jnp.where(first_map, q, q_pos).T.astype(BF16)
    qb_ref[...] = jnp.where(first_map, q_pos, q).T.astype(BF16)
    k = proj(4)
    ka_ref[...] = jnp.where(first_map, k, k_pos).astype(BF16)
    kb_ref[...] = jnp.where(first_map, k_pos, k).astype(BF16)
    vt = proj(5).T.astype(BF16)
    ones_row = (lax.broadcasted_iota(jnp.int32, (SUM_ROWS, ROW_TILE), 0) == 0).astype(BF16)
    for hd in range(ATT_HEADS):
        vt_ref[hd * V_AUG:hd * V_AUG + V_DIM, :] = vt[hd * V_DIM:(hd + 1) * V_DIM]
        vt_ref[hd * V_AUG + V_DIM:(hd + 1) * V_AUG, :] = ones_row


def _position_coefficients():
    lane = jnp.arange(D_ATT)
    e = lane % HEAD_DIM
    slope = 2.0 ** (-8.0 * ((lane // V_DIM).astype(F32) + 1.0) / ATT_HEADS)
    pieces = []
    rest = np.float64(LOG2_E)
    for _ in range(3):
        piece = np.float64(rest.astype(BF16))
        pieces.append(piece)
        rest = rest - piece
    log2e_piece = jnp.asarray(pieces, F32)[e % 3]
    zero = jnp.zeros((D_ATT,), F32)
    rows = [
        jnp.where(e < 3, slope * POS_SPLIT, 0.0),
        jnp.where((e >= 3) & (e < 6), slope, 0.0),
        jnp.where((e >= 6) & (e < 12), log2e_piece, 0.0),
        jnp.where((e >= 6) & (e < 9), -slope * POS_SPLIT, 0.0),
        jnp.where((e >= 9) & (e < 12), -slope, 0.0),
        jnp.where(e < 6, log2e_piece, 0.0),
        zero, zero,
    ]
    return jnp.stack(rows).astype(F32)


def _in_proj(x2, g, coef, w, *, seq):
    t = x2.shape[0]
    grid = (t // ROW_TILE,)
    row = lambda i: (i, 0)
    col = lambda i: (0, i)
    const = lambda i: (0, 0)
    return pl.pallas_call(
        functools.partial(_in_proj_kernel, tiles_per_seq=seq // ROW_TILE),
        grid=grid,
        in_specs=[
            pl.BlockSpec((ROW_TILE, D_MODEL), row),
            pl.BlockSpec((1, D_MODEL), const),
            pl.BlockSpec((8, D_ATT), const),
            pl.BlockSpec((D_MODEL, D_MIX_IN), const, pipeline_mode=pl.Buffered(1)),
        ],
        out_specs=[
            pl.BlockSpec((ROW_TILE, D_CONV), row),
            pl.BlockSpec((ROW_TILE, D_CONV), row),
            pl.BlockSpec((ROW_TILE, D_ATT), row),
            pl.BlockSpec((ROW_TILE, D_ATT), row),
            pl.BlockSpec((D_ATT, ROW_TILE), col),
            pl.BlockSpec((D_ATT, ROW_TILE), col),
            pl.BlockSpec((ATT_HEADS * V_AUG, ROW_TILE), col),
        ],
        out_shape=[
            jax.ShapeDtypeStruct((t, D_CONV), F32),
            jax.ShapeDtypeStruct((t, D_CONV), F32),
            jax.ShapeDtypeStruct((t, D_ATT), BF16),
            jax.ShapeDtypeStruct((t, D_ATT), BF16),
            jax.ShapeDtypeStruct((D_ATT, t), BF16),
            jax.ShapeDtypeStruct((D_ATT, t), BF16),
            jax.ShapeDtypeStruct((ATT_HEADS * V_AUG, t), BF16),
        ],
        compiler_params=pltpu.CompilerParams(
            dimension_semantics=("arbitrary",), vmem_limit_bytes=VMEM_LIMIT),
        name="in_proj",
    )(x2, g, coef, w)


def _attn_kernel(slopes_ref, lam_ref, qa_ref, qb_ref, ka_ref, kb_ref, vt_ref, g_ref, o_ref,
                 m_ref, acc_ref, s0_ref, s1_ref, smax0_ref, smax1_ref, corr_ref, reach_ref,
                 *, seq, lambda_init):
    head = pl.program_id(1)
    nk = seq // K_TILE
    decay = LOG2_E * slopes_ref[head]

    reach_ref[0] = nk - 1

    @pl.when(decay * (K_TILE * (nk - 2) + 1) > UNDERFLOW_LOG2)
    def _():
        def max_norm2(ref, feature_axis, first_half):
            x = ref[...].astype(F32)
            idx = lax.broadcasted_iota(jnp.int32, x.shape, feature_axis)
            feature = (idx < HEAD_DIM) if first_half else (idx >= HEAD_DIM)
            n2 = jnp.sum(jnp.where(feature, x * x, 0.0), axis=feature_axis, keepdims=True)
            return jnp.max(n2, axis=1 - feature_axis, keepdims=True)

        q2 = jnp.maximum(max_norm2(qa_ref, 0, True), max_norm2(qb_ref, 0, False))
        k2 = jnp.maximum(max_norm2(ka_ref, 1, True), max_norm2(kb_ref, 1, False))
        budget = 2.02 * jnp.sqrt(q2 * k2) + UNDERFLOW_LOG2
        reach = jnp.floor((budget / decay - 1.0) * (1.0 / K_TILE)) + 1.0
        reach = jnp.where(reach < nk - 1, jnp.maximum(reach, 0.0), nk - 1.0)
        reach_ref[0] = reach.astype(jnp.int32)[0, 0]

    reach = reach_ref[0]

    key_minus_query = (lax.broadcasted_iota(jnp.int32, (K_TILE, Q_TILE), 0)
                       - lax.broadcasted_iota(jnp.int32, (K_TILE, Q_TILE), 1))
    corr_ref[...] = (-2.0 * decay) * jnp.maximum(key_minus_query, 0).astype(F32)

    lv = lam_ref[...]
    lam = (jnp.exp(jnp.sum(lv[0:1] * lv[1:2], axis=-1, keepdims=True))
           - jnp.exp(jnp.sum(lv[2:3] * lv[3:4], axis=-1, keepdims=True)) + lambda_init)
    buf0 = (s0_ref, smax0_ref)
    buf1 = (s1_ref, smax1_ref)

    def store_scores(s, buf, mp):
        s_ref, smax_ref = buf
        s_ref[mp] = s
        smax_ref[mp] = jnp.max(s, axis=0, keepdims=True)

    def diagonal_scores(qi, buf):
        t0 = pl.multiple_of(qi * Q_TILE, Q_TILE)
        for mp, (k_ref, q_ref) in enumerate(((ka_ref, qa_ref), (kb_ref, qb_ref))):
            s = jnp.dot(k_ref[pl.ds(t0, K_TILE), :], q_ref[:, pl.ds(t0, Q_TILE)],
                        preferred_element_type=F32)
            store_scores(s + corr_ref[...], buf, mp)

    diagonal_scores(0, buf0)

    def query_tile(qi, carry):
        q0 = pl.multiple_of(qi * Q_TILE, Q_TILE)
        row = lax.broadcasted_iota(jnp.int32, (V_DIM, Q_TILE), 0)
        qa_before = qa_ref[:, pl.ds(q0, Q_TILE)]
        qb_before = qb_ref[:, pl.ds(q0, Q_TILE)]
        qa_after = jnp.where(row >= HEAD_DIM, -qa_before, qa_before)
        qb_after = jnp.where(row < HEAD_DIM, -qb_before, qb_before)

        m_ref[...] = jnp.full(m_ref.shape, -1e30, F32)
        acc_ref[...] = jnp.zeros(acc_ref.shape, F32)

        lo = jnp.maximum(qi - reach, 0)
        hi = jnp.minimum(qi + reach, nk - 1)
        odd = (hi - lo + 1) % 2
        grow_hi = jnp.where(hi < nk - 1, odd, 0)
        hi = hi + grow_hi
        lo = lo - (odd - grow_hi)
        visits = hi - lo + 1

        def tile_of(n):
            j = lo + n - 1
            return jnp.where(n == 0, qi, jnp.where(j < qi, j, j + 1))

        def scores(n, buf):
            j = tile_of(n)
            k0 = pl.multiple_of(j * K_TILE, K_TILE)
            keys_before = j < qi
            for mp, (k_ref, q_before, q_after) in enumerate(
                    ((ka_ref, qa_before, qa_after), (kb_ref, qb_before, qb_after))):
                qt = jnp.where(keys_before, q_before, q_after)
                s = jnp.dot(k_ref[pl.ds(k0, K_TILE), :], qt, preferred_element_type=F32)
                store_scores(s, buf, mp)

        def absorb(n, buf):
            s_ref, smax_ref = buf
            k0 = pl.multiple_of(tile_of(n) * K_TILE, K_TILE)
            vt = vt_ref[:, pl.ds(k0, K_TILE)]
            for mp in range(2):
                s = s_ref[mp]
                m_old = m_ref[mp]
                m_new = jnp.maximum(m_old, smax_ref[mp])
                alpha = jnp.exp2(m_old - m_new)
                p = jnp.exp2(s - m_new)
                acc_ref[mp] = alpha * acc_ref[mp] + jnp.dot(vt, p.astype(BF16),
                                                            preferred_element_type=F32)
                m_ref[mp] = m_new

        def pipeline_steps(first, count):
            for offset in range(0, count, 2):
                n = first + offset
                scores(n + 1, buf1)
                absorb(n, buf0)
                scores(n + 2, buf0)
                absorb(n + 1, buf1)

        def quad(p, c):
            pipeline_steps(4 * p, 4)
            return c

        quads = (visits - 2) // 4
        lax.fori_loop(0, quads, quad, 0)

        def pair(p, c):
            pipeline_steps(4 * quads + 2 * p, 2)
            return c

        lax.fori_loop(0, (visits - 2 - 4 * quads) // 2, pair, 0)
        scores(visits - 1, buf1)
        absorb(visits - 2, buf0)
        diagonal_scores(jnp.minimum(qi + 1, seq // Q_TILE - 1), buf0)
        absorb(visits - 1, buf1)

        def normalised(mp):
            a = acc_ref[mp]
            return a[:V_DIM] / a[V_DIM:V_DIM + 1]

        o = normalised(0) - lam * normalised(1)
        y = o * lax.rsqrt(jnp.mean(o * o, axis=0, keepdims=True) + SUBLN_EPS) * g_ref[...]
        o_ref[pl.ds(q0, Q_TILE), :] = (y * (1.0 - lambda_init)).T.astype(BF16)
        return carry

    lax.fori_loop(0, seq // Q_TILE, query_tile, 0)


def _attention(slopes, lam, qa, qb, ka, kb, vt, g_col, *, batch, seq, lambda_init):
    assert Q_TILE == K_TILE and (seq // K_TILE) % 2 == 0
    assert 8 % ATT_HEADS == 0
    t = batch * seq
    rows = lambda b, h: (b, h)
    cols = lambda b, h: (h, b)
    const = lambda b, h: (0, 0)
    return pl.pallas_call(
        functools.partial(_attn_kernel, seq=seq, lambda_init=lambda_init),
        grid=(batch, ATT_HEADS),
        in_specs=[
            pl.BlockSpec(memory_space=pltpu.SMEM),
            pl.BlockSpec((4, HEAD_DIM), const),
            pl.BlockSpec((V_DIM, seq), cols),
            pl.BlockSpec((V_DIM, seq), cols),
            pl.BlockSpec((seq, V_DIM), rows),
            pl.BlockSpec((seq, V_DIM), rows),
            pl.BlockSpec((V_AUG, seq), cols),
            pl.BlockSpec((V_DIM, 1), const),
        ],
        out_specs=pl.BlockSpec((seq, V_DIM), rows),
        out_shape=jax.ShapeDtypeStruct((t, D_ATT), BF16),
        scratch_shapes=[
            pltpu.VMEM((2, 1, Q_TILE), F32),
            pltpu.VMEM((2, V_AUG, Q_TILE), F32),
            pltpu.VMEM((2, K_TILE, Q_TILE), F32),
            pltpu.VMEM((2, K_TILE, Q_TILE), F32),
            pltpu.VMEM((2, 1, Q_TILE), F32),
            pltpu.VMEM((2, 1, Q_TILE), F32),
            pltpu.VMEM((K_TILE, Q_TILE), F32),
            pltpu.SMEM((1,), jnp.int32),
        ],
        compiler_params=pltpu.CompilerParams(
            dimension_semantics=("arbitrary", "arbitrary"), vmem_limit_bytes=VMEM_LIMIT),
        name="diff_attention",
    )(slopes, lam, qa, qb, ka, kb, vt, g_col)


def _out_proj_kernel(b_ref, ch_ref, chp_ref, chn_ref, ya_ref, cw_ref, gc_ref, w_ref,
                     x_ref, gp_ref, o_ref, *, tiles_per_seq):
    i = pl.program_id(0)
    has_prev = (i % tiles_per_seq != 0).astype(F32)
    has_next = (i % tiles_per_seq != tiles_per_seq - 1).astype(F32)
    ch = ch_ref[...]
    rows = lax.broadcasted_iota(jnp.int32, ch.shape, 0)
    prev = jnp.where(rows == 0, chp_ref[CONV_HALO - 1:CONV_HALO, :] * has_prev,
                     pltpu.roll(ch, 1, 0))
    nxt = jnp.where(rows == ROW_TILE - 1, chn_ref[0:1, :] * has_next,
                    pltpu.roll(ch, ROW_TILE - 1, 0))
    cw = cw_ref[...]
    y = b_ref[...] * (prev * cw[0:1] + ch * cw[1:2] + nxt * cw[2:3])
    z = y * y
    first_group = lax.broadcasted_iota(jnp.int32, (ROW_TILE, 2 * CONV_GROUP_DIM), 1) < CONV_GROUP_DIM
    ms_blocks = []
    for blk in range(D_CONV // (2 * CONV_GROUP_DIM)):
        zb = z[:, blk * 2 * CONV_GROUP_DIM:(blk + 1) * 2 * CONV_GROUP_DIM]
        s_first = jnp.sum(jnp.where(first_group, zb, 0.0), axis=1, keepdims=True)
        s_second = jnp.sum(jnp.where(first_group, 0.0, zb), axis=1, keepdims=True)
        ms_blocks.append(jnp.where(first_group, s_first, s_second))
    ms = jnp.concatenate(ms_blocks, axis=1) * (1.0 / CONV_GROUP_DIM)
    yc = (y * lax.rsqrt(ms + NORM_EPS) * gc_ref[...]).astype(BF16)
    mix = (jnp.dot(yc, w_ref[0:D_CONV, :], preferred_element_type=F32)
           + jnp.dot(ya_ref[...], w_ref[D_CONV:, :], preferred_element_type=F32))
    o_ref[...] = x_ref[...] + _rms_norm(mix, gp_ref[...], NORM_EPS)


def _out_proj(b, ch, ya, cw, gc, w, x2, gp, *, seq):
    t = x2.shape[0]
    n = t // ROW_TILE
    hb = ROW_TILE // CONV_HALO
    n_halo = t // CONV_HALO
    row = lambda i: (i, 0)
    const = lambda i: (0, 0)
    return pl.pallas_call(
        functools.partial(_out_proj_kernel, tiles_per_seq=seq // ROW_TILE),
        grid=(n,),
        in_specs=[
            pl.BlockSpec((ROW_TILE, D_CONV), row),
            pl.BlockSpec((ROW_TILE, D_CONV), row),
            pl.BlockSpec((CONV_HALO, D_CONV), lambda i: (jnp.maximum(i * hb - 1, 0), 0)),
            pl.BlockSpec((CONV_HALO, D_CONV), lambda i: (jnp.minimum((i + 1) * hb, n_halo - 1), 0)),
            pl.BlockSpec((ROW_TILE, D_ATT), row),
            pl.BlockSpec((3, D_CONV), const),
            pl.BlockSpec((1, D_CONV), const),
            pl.BlockSpec((D_MODEL, D_MODEL), const, pipeline_mode=pl.Buffered(1)),
            pl.BlockSpec((ROW_TILE, D_MODEL), row),
            pl.BlockSpec((1, D_MODEL), const),
        ],
        out_specs=pl.BlockSpec((ROW_TILE, D_MODEL), row),
        out_shape=jax.ShapeDtypeStruct((t, D_MODEL), F32),
        compiler_params=pltpu.CompilerParams(
            dimension_semantics=("arbitrary",), vmem_limit_bytes=VMEM_LIMIT),
        name="out_proj",
    )(b, ch, ch, ch, ya, cw, gc, w, x2, gp)


def _ffn_kernel(x_ref, xp_ref, xn_ref, g_ref, wu_ref, cw_ref, cb_ref, wd_ref, gp_ref, o_ref,
                hs_ref, act_ref, *, tiles_per_seq):
    i = pl.program_id(0)
    has_prev = (i % tiles_per_seq != 0).astype(F32)
    has_next = (i % tiles_per_seq != tiles_per_seq - 1).astype(F32)
    g = g_ref[...]
    x = x_ref[...]
    halo_row = lax.broadcasted_iota(jnp.int32, (FFN_HALO, D_MODEL), 0)
    h_prev = _rms_norm(xp_ref[...], g, NORM_EPS) * has_prev
    h_next = _rms_norm(xn_ref[...], g, NORM_EPS) * has_next
    hs_ref[0:FFN_HALO, :] = jnp.where(halo_row == FFN_HALO - 1, h_prev,
                                      jnp.where(halo_row == 0, h_next, 0.0)).astype(BF16)
    hs_ref[FFN_HALO:, :] = _rms_norm(x, g, NORM_EPS).astype(BF16)
    hs = hs_ref[...]
    ext = FFN_ROW_TILE + FFN_HALO

    def conv_chunk(c0):
        cols = slice(c0, c0 + FFN_CHUNK)
        u = jnp.dot(hs, wu_ref[:, cols], preferred_element_type=F32)
        cw = cw_ref[:, cols]
        mid = slice(FFN_HALO, FFN_HALO + FFN_ROW_TILE)
        return (pltpu.roll(u, 1, 0)[mid] * cw[0:1] + u[mid] * cw[1:2]
                + pltpu.roll(u, ext - 1, 0)[mid] * cw[2:3] + cb_ref[:, cols])

    for c in range(D_FF // FFN_CHUNK):
        gate = conv_chunk(c * FFN_CHUNK)
        up = conv_chunk(D_FF + c * FFN_CHUNK)
        act = gate * (1.0 / (1.0 + jnp.exp(-gate))) * up
        act_ref[:, c * FFN_CHUNK:(c + 1) * FFN_CHUNK] = act.astype(BF16)

    f = jnp.dot(act_ref[...], wd_ref[...], preferred_element_type=F32)
    o_ref[...] = x + _rms_norm(f, gp_ref[...], NORM_EPS)


def _ffn(x1, g, wu, cw, cb, wd, gp, *, seq):
    t = x1.shape[0]
    n = t // FFN_ROW_TILE
    hb = FFN_ROW_TILE // FFN_HALO
    n_halo = t // FFN_HALO
    row = lambda i: (i, 0)
    const = lambda i: (0, 0)
    return pl.pallas_call(
        functools.partial(_ffn_kernel, tiles_per_seq=seq // FFN_ROW_TILE),
        grid=(n,),
        in_specs=[
            pl.BlockSpec((FFN_ROW_TILE, D_MODEL), row),
            pl.BlockSpec((FFN_HALO, D_MODEL), lambda i: (jnp.maximum(i * hb - 1, 0), 0)),
            pl.BlockSpec((FFN_HALO, D_MODEL), lambda i: (jnp.minimum((i + 1) * hb, n_halo - 1), 0)),
            pl.BlockSpec((1, D_MODEL), const),
            pl.BlockSpec((D_MODEL, 2 * D_FF), const, pipeline_mode=pl.Buffered(1)),
            pl.BlockSpec((3, 2 * D_FF), const),
            pl.BlockSpec((1, 2 * D_FF), const),
            pl.BlockSpec((D_FF, D_MODEL), const, pipeline_mode=pl.Buffered(1)),
            pl.BlockSpec((1, D_MODEL), const),
        ],
        out_specs=pl.BlockSpec((FFN_ROW_TILE, D_MODEL), row),
        out_shape=jax.ShapeDtypeStruct((t, D_MODEL), F32),
        scratch_shapes=[
            pltpu.VMEM((FFN_ROW_TILE + FFN_HALO, D_MODEL), BF16),
            pltpu.VMEM((FFN_ROW_TILE, D_FF), BF16),
        ],
        compiler_params=pltpu.CompilerParams(
            dimension_semantics=("arbitrary",), vmem_limit_bytes=VMEM_LIMIT),
        name="conv_glu_ffn",
    )(x1, x1, x1, g, wu, cw, cb, wd, gp)


def kernel(x, g_mix_pre, w_mix_in, conv_w, g_conv_out, lambda_q1, lambda_k1, lambda_q2, lambda_k2,
           g_subln, w_mix_out, g_mix_post, g_ffn_pre, w_ffn_up, ffn_conv_w, ffn_conv_b, w_ffn_down,
           g_ffn_post):
    batch, seq, d = x.shape
    depth = g_mix_pre.shape[0]
    assert d == D_MODEL and seq % ROW_TILE == 0 and seq % Q_TILE == 0 and seq % K_TILE == 0
    assert seq % FFN_ROW_TILE == 0
    assert seq <= POS_SPLIT * POS_SPLIT
    t = batch * seq
    x2 = x.reshape(t, d)
    slopes = 2.0 ** (-8.0 * (jnp.arange(ATT_HEADS, dtype=F32) + 1.0) / ATT_HEADS)
    coef = _position_coefficients()

    for layer in range(depth):
        lambda_init = 0.8 - 0.6 * math.exp(-0.3 * layer)
        lam = jnp.stack([lambda_q1[layer], lambda_k1[layer], lambda_q2[layer], lambda_k2[layer]]).astype(F32)
        b, ch, ka, kb, qa, qb, vt = _in_proj(x2, g_mix_pre[layer][None], coef,
                                             w_mix_in[layer].astype(BF16), seq=seq)
        ya = _attention(slopes, lam, qa, qb, ka, kb, vt, g_subln[layer][:, None],
                        batch=batch, seq=seq, lambda_init=lambda_init)
        x2 = _out_proj(b, ch, ya, conv_w[layer], g_conv_out[layer][None],
                       w_mix_out[layer].astype(BF16), x2, g_mix_post[layer][None], seq=seq)
        x2 = _ffn(x2, g_ffn_pre[layer][None], w_ffn_up[layer].astype(BF16), ffn_conv_w[layer],
                  ffn_conv_b[layer][None], w_ffn_down[layer].astype(BF16), g_ffn_post[layer][None],
                  seq=seq)
    return x2.reshape(batch, seq, d)
```

```python
import functools
import math

import jax
import jax.numpy as jnp
import numpy as np
from jax import lax
from jax.experimental import pallas as pl
from jax.experimental.pallas import tpu as pltpu

F32 = jnp.float32
BF16 = jnp.bfloat16

D_MODEL = 1024
D_CONV = 512
D_ATT = 512
CONV_GROUP_DIM = 64
HEAD_DIM = 64
V_DIM = 2 * HEAD_DIM
SUM_ROWS = 16
V_AUG = V_DIM + SUM_ROWS
ATT_HEADS = D_ATT // V_DIM
D_MIX_IN = 3 * D_CONV + 3 * D_ATT
D_FF = 2816
NORM_EPS = 1e-6
SUBLN_EPS = 1e-5
QK_SCALE = HEAD_DIM ** -0.5
LOG2_E = math.log2(math.e)

ROW_TILE = 1024
Q_TILE = 512
K_TILE = 512
FFN_ROW_TILE = 1024
FFN_HALO = 16
CONV_HALO = 8
FFN_CHUNK = 256
POS_SPLIT = 64
PIPELINE_BODY_STEPS = (6, 4, 2)
UNDERFLOW_LOG2 = 152.0
VMEM_LIMIT = 56 * 1024 * 1024


def _rms_norm(x, g, eps):
    return x * lax.rsqrt(jnp.mean(x * x, axis=-1, keepdims=True) + eps) * g


def _in_proj_kernel(x_ref, g_ref, coef_ref, w_ref, b_ref, ch_ref, ka_ref, kb_ref, qa_ref, qb_ref,
                    vt_ref, *, tiles_per_seq):
    h = _rms_norm(x_ref[...], g_ref[...], NORM_EPS).astype(BF16)

    def proj(col):
        return jnp.dot(h, w_ref[:, col * D_CONV:(col + 1) * D_CONV], preferred_element_type=F32)

    b_ref[...] = proj(0)
    ch_ref[...] = proj(1) * proj(2)

    pos = ((pl.program_id(0) % tiles_per_seq) * ROW_TILE
           + lax.broadcasted_iota(jnp.int32, (ROW_TILE, 1), 0))
    hi = (pos // POS_SPLIT).astype(F32)
    lo = (pos % POS_SPLIT).astype(F32)
    coef = coef_ref[...]
    k_pos = hi * coef[0:1] + lo * coef[1:2] + coef[2:3]
    q_pos = hi * coef[3:4] + lo * coef[4:5] + coef[5:6]
    lane = lax.broadcasted_iota(jnp.int32, (ROW_TILE, D_ATT), 1)
    first_map = (lane & HEAD_DIM) == 0
    q = proj(3) * (QK_SCALE * LOG2_E)
    qa_ref[...] = jnp.where(first_map, q, q_pos).T.astype(BF16)
    qb_ref[...] = jnp.where(first_map, q_pos, q).T.astype(BF16)
    k = proj(4)
    ka_ref[...] = jnp.where(first_map, k, k_pos).astype(BF16)
    kb_ref[...] = jnp.where(first_map, k_pos, k).astype(BF16)
    vt = proj(5).T.astype(BF16)
    ones_row = (lax.broadcasted_iota(jnp.int32, (SUM_ROWS, ROW_TILE), 0) == 0).astype(BF16)
    for hd in range(ATT_HEADS):
        vt_ref[hd * V_AUG:hd * V_AUG + V_DIM, :] = vt[hd * V_DIM:(hd + 1) * V_DIM]
        vt_ref[hd * V_AUG + V_DIM:(hd + 1) * V_AUG, :] = ones_row


def _position_coefficients():
    lane = jnp.arange(D_ATT)
    e = lane % HEAD_DIM
    slope = 2.0 ** (-8.0 * ((lane // V_DIM).astype(F32) + 1.0) / ATT_HEADS)
    pieces = []
    rest = np.float64(LOG2_E)
    for _ in range(3):
        piece = np.float64(rest.astype(BF16))
        pieces.append(piece)
        rest = rest - piece
    log2e_piece = jnp.asarray(pieces, F32)[e % 3]
    zero = jnp.zeros((D_ATT,), F32)
    rows = [
        jnp.where(e < 3, slope * POS_SPLIT, 0.0),
        jnp.where((e >= 3) & (e < 6), slope, 0.0),
        jnp.where((e >= 6) & (e < 12), log2e_piece, 0.0),
        jnp.where((e >= 6) & (e < 9), -slope * POS_SPLIT, 0.0),
        jnp.where((e >= 9) & (e < 12), -slope, 0.0),
        jnp.where(e < 6, log2e_piece, 0.0),
        zero, zero,
    ]
    return jnp.stack(rows).astype(F32)


def _in_proj(x2, g, coef, w, *, seq):
    t = x2.shape[0]
    grid = (t // ROW_TILE,)
    row = lambda i: (i, 0)
    col = lambda i: (0, i)
    const = lambda i: (0, 0)
    return pl.pallas_call(
        functools.partial(_in_proj_kernel, tiles_per_seq=seq // ROW_TILE),
        grid=grid,
        in_specs=[
            pl.BlockSpec((ROW_TILE, D_MODEL), row),
            pl.BlockSpec((1, D_MODEL), const),
            pl.BlockSpec((8, D_ATT), const),
            pl.BlockSpec((D_MODEL, D_MIX_IN), const, pipeline_mode=pl.Buffered(1)),
        ],
        out_specs=[
            pl.BlockSpec((ROW_TILE, D_CONV), row),
            pl.BlockSpec((ROW_TILE, D_CONV), row),
            pl.BlockSpec((ROW_TILE, D_ATT), row),
            pl.BlockSpec((ROW_TILE, D_ATT), row),
            pl.BlockSpec((D_ATT, ROW_TILE), col),
            pl.BlockSpec((D_ATT, ROW_TILE), col),
            pl.BlockSpec((ATT_HEADS * V_AUG, ROW_TILE), col),
        ],
        out_shape=[
            jax.ShapeDtypeStruct((t, D_CONV), F32),
            jax.ShapeDtypeStruct((t, D_CONV), F32),
            jax.ShapeDtypeStruct((t, D_ATT), BF16),
            jax.ShapeDtypeStruct((t, D_ATT), BF16),
            jax.ShapeDtypeStruct((D_ATT, t), BF16),
            jax.ShapeDtypeStruct((D_ATT, t), BF16),
            jax.ShapeDtypeStruct((ATT_HEADS * V_AUG, t), BF16),
        ],
        compiler_params=pltpu.CompilerParams(
            dimension_semantics=("arbitrary",), vmem_limit_bytes=VMEM_LIMIT),
        name="in_proj",
    )(x2, g, coef, w)


def _attn_kernel(slopes_ref, lam_ref, qa_ref, qb_ref, ka_ref, kb_ref, vt_ref, g_ref, o_ref,
                 m_ref, acc_ref, s0_ref, s1_ref, smax0_ref, smax1_ref, corr_ref, reach_ref,
                 *, seq, lambda_init):
    head = pl.program_id(1)
    nk = seq // K_TILE
    decay = LOG2_E * slopes_ref[head]

    reach_ref[0] = nk - 1

    @pl.when(decay * (K_TILE * (nk - 2) + 1) > UNDERFLOW_LOG2)
    def _():
        def max_norm2(ref, feature_axis, first_half):
            x = ref[...].astype(F32)
            idx = lax.broadcasted_iota(jnp.int32, x.shape, feature_axis)
            feature = (idx < HEAD_DIM) if first_half else (idx >= HEAD_DIM)
            n2 = jnp.sum(jnp.where(feature, x * x, 0.0), axis=feature_axis, keepdims=True)
            return jnp.max(n2, axis=1 - feature_axis, keepdims=True)

        q2 = jnp.maximum(max_norm2(qa_ref, 0, True), max_norm2(qb_ref, 0, False))
        k2 = jnp.maximum(max_norm2(ka_ref, 1, True), max_norm2(kb_ref, 1, False))
        budget = 2.02 * jnp.sqrt(q2 * k2) + UNDERFLOW_LOG2
        reach = jnp.floor((budget / decay - 1.0) * (1.0 / K_TILE)) + 1.0
        reach = jnp.where(reach < nk - 1, jnp.maximum(reach, 0.0), nk - 1.0)
        reach_ref[0] = reach.astype(jnp.int32)[0, 0]

    reach = reach_ref[0]

    key_minus_query = (lax.broadcasted_iota(jnp.int32, (K_TILE, Q_TILE), 0)
                       - lax.broadcasted_iota(jnp.int32, (K_TILE, Q_TILE), 1))
    corr_ref[...] = (-2.0 * decay) * jnp.maximum(key_minus_query, 0).astype(F32)

    lv = lam_ref[...]
    lam = (jnp.exp(jnp.sum(lv[0:1] * lv[1:2], axis=-1, keepdims=True))
           - jnp.exp(jnp.sum(lv[2:3] * lv[3:4], axis=-1, keepdims=True)) + lambda_init)
    buf0 = (s0_ref, smax0_ref)
    buf1 = (s1_ref, smax1_ref)

    def store_scores(s, buf, mp):
        s_ref, smax_ref = buf
        s_ref[mp] = s
        smax_ref[mp] = jnp.max(s, axis=0, keepdims=True)

    def diagonal_scores(qi, buf):
        t0 = pl.multiple_of(qi * Q_TILE, Q_TILE)
        for mp, (k_ref, q_ref) in enumerate(((ka_ref, qa_ref), (kb_ref, qb_ref))):
            s = jnp.dot(k_ref[pl.ds(t0, K_TILE), :], q_ref[:, pl.ds(t0, Q_TILE)],
                        preferred_element_type=F32)
            store_scores(s + corr_ref[...], buf, mp)

    diagonal_scores(0, buf0)

    def query_tile(qi, carry):
        q0 = pl.multiple_of(qi * Q_TILE, Q_TILE)
        row = lax.broadcasted_iota(jnp.int32, (V_DIM, Q_TILE), 0)
        qa_before = qa_ref[:, pl.ds(q0, Q_TILE)]
        qb_before = qb_ref[:, pl.ds(q0, Q_TILE)]
        qa_after = jnp.where(row >= HEAD_DIM, -qa_before, qa_before)
        qb_after = jnp.where(row < HEAD_DIM, -qb_before, qb_before)

        m_ref[...] = jnp.full(m_ref.shape, -1e30, F32)
        acc_ref[...] = jnp.zeros(acc_ref.shape, F32)

        lo = jnp.maximum(qi - reach, 0)
        hi = jnp.minimum(qi + reach, nk - 1)
        odd = (hi - lo + 1) % 2
        grow_hi = jnp.where(hi < nk - 1, odd, 0)
        hi = hi + grow_hi
        lo = lo - (odd - grow_hi)
        visits = hi - lo + 1

        def tile_of(n):
            j = lo + n - 1
            return jnp.where(n == 0, qi, jnp.where(j < qi, j, j + 1))

        def scores(n, buf):
            j = tile_of(n)
            k0 = pl.multiple_of(j * K_TILE, K_TILE)
            keys_before = j < qi
            for mp, (k_ref, q_before, q_after) in enumerate(
                    ((ka_ref, qa_before, qa_after), (kb_ref, qb_before, qb_after))):
                qt = jnp.where(keys_before, q_before, q_after)
                s = jnp.dot(k_ref[pl.ds(k0, K_TILE), :], qt, preferred_element_type=F32)
                store_scores(s, buf, mp)

        def absorb(n, buf):
            s_ref, smax_ref = buf
            k0 = pl.multiple_of(tile_of(n) * K_TILE, K_TILE)
            vt = vt_ref[:, pl.ds(k0, K_TILE)]
            for mp in range(2):
                s = s_ref[mp]
                m_old = m_ref[mp]
                m_new = jnp.maximum(m_old, smax_ref[mp])
                alpha = jnp.exp2(m_old - m_new)
                p = jnp.exp2(s - m_new)
                acc_ref[mp] = alpha * acc_ref[mp] + jnp.dot(vt, p.astype(BF16),
                                                            preferred_element_type=F32)
                m_ref[mp] = m_new

        def pipeline_steps(first, count):
            for offset in range(0, count, 2):
                n = first + offset
                scores(n + 1, buf1)
                absorb(n, buf0)
                scores(n + 2, buf0)
                absorb(n + 1, buf1)

        done = jnp.int32(0)
        for body_steps in PIPELINE_BODY_STEPS:
            trips = (visits - 2 - done) // body_steps

            def body(p, c, start=done, body_steps=body_steps):
                pipeline_steps(start + body_steps * p, body_steps)
                return c

            lax.fori_loop(0, trips, body, 0)
            done = done + trips * body_steps
        scores(visits - 1, buf1)
        absorb(visits - 2, buf0)
        diagonal_scores(jnp.minimum(qi + 1, seq // Q_TILE - 1), buf0)
        absorb(visits - 1, buf1)

        def normalised(mp):
            a = acc_ref[mp]
            return a[:V_DIM] / a[V_DIM:V_DIM + 1]

        o = normalised(0) - lam * normalised(1)
        y = o * lax.rsqrt(jnp.mean(o * o, axis=0, keepdims=True) + SUBLN_EPS) * g_ref[...]
        o_ref[pl.ds(q0, Q_TILE), :] = (y * (1.0 - lambda_init)).T.astype(BF16)
        return carry

    lax.fori_loop(0, seq // Q_TILE, query_tile, 0)


def _attention(slopes, lam, qa, qb, ka, kb, vt, g_col, *, batch, seq, lambda_init):
    assert Q_TILE == K_TILE and (seq // K_TILE) % 2 == 0
    assert 8 % ATT_HEADS == 0
    t = batch * seq
    rows = lambda b, h: (b, h)
    cols = lambda b, h: (h, b)
    const = lambda b, h: (0, 0)
    return pl.pallas_call(
        functools.partial(_attn_kernel, seq=seq, lambda_init=lambda_init),
        grid=(batch, ATT_HEADS),
        in_specs=[
            pl.BlockSpec(memory_space=pltpu.SMEM),
            pl.BlockSpec((4, HEAD_DIM), const),
            pl.BlockSpec((V_DIM, seq), cols),
            pl.BlockSpec((V_DIM, seq), cols),
            pl.BlockSpec((seq, V_DIM), rows),
            pl.BlockSpec((seq, V_DIM), rows),
            pl.BlockSpec((V_AUG, seq), cols),
            pl.BlockSpec((V_DIM, 1), const),
        ],
        out_specs=pl.BlockSpec((seq, V_DIM), rows),
        out_shape=jax.ShapeDtypeStruct((t, D_ATT), BF16),
        scratch_shapes=[
            pltpu.VMEM((2, 1, Q_TILE), F32),
            pltpu.VMEM((2, V_AUG, Q_TILE), F32),
            pltpu.VMEM((2, K_TILE, Q_TILE), F32),
            pltpu.VMEM((2, K_TILE, Q_TILE), F32),
            pltpu.VMEM((2, 1, Q_TILE), F32),
            pltpu.VMEM((2, 1, Q_TILE), F32),
            pltpu.VMEM((K_TILE, Q_TILE), F32),
            pltpu.SMEM((1,), jnp.int32),
        ],
        compiler_params=pltpu.CompilerParams(
            dimension_semantics=("arbitrary", "arbitrary"), vmem_limit_bytes=VMEM_LIMIT),
        name="diff_attention",
    )(slopes, lam, qa, qb, ka, kb, vt, g_col)


def _out_proj_kernel(b_ref, ch_ref, chp_ref, chn_ref, ya_ref, cw_ref, gc_ref, w_ref,
                     x_ref, gp_ref, o_ref, *, tiles_per_seq):
    i = pl.program_id(0)
    has_prev = (i % tiles_per_seq != 0).astype(F32)
    has_next = (i % tiles_per_seq != tiles_per_seq - 1).astype(F32)
    ch = ch_ref[...]
    rows = lax.broadcasted_iota(jnp.int32, ch.shape, 0)
    prev = jnp.where(rows == 0, chp_ref[CONV_HALO - 1:CONV_HALO, :] * has_prev,
                     pltpu.roll(ch, 1, 0))
    nxt = jnp.where(rows == ROW_TILE - 1, chn_ref[0:1, :] * has_next,
                    pltpu.roll(ch, ROW_TILE - 1, 0))
    cw = cw_ref[...]
    y = b_ref[...] * (prev * cw[0:1] + ch * cw[1:2] + nxt * cw[2:3])
    z = y * y
    first_group = lax.broadcasted_iota(jnp.int32, (ROW_TILE, 2 * CONV_GROUP_DIM), 1) < CONV_GROUP_DIM
    ms_blocks = []
    for blk in range(D_CONV // (2 * CONV_GROUP_DIM)):
        zb = z[:, blk * 2 * CONV_GROUP_DIM:(blk + 1) * 2 * CONV_GROUP_DIM]
        s_first = jnp.sum(jnp.where(first_group, zb, 0.0), axis=1, keepdims=True)
        s_second = jnp.sum(jnp.where(first_group, 0.0, zb), axis=1, keepdims=True)
        ms_blocks.append(jnp.where(first_group, s_first, s_second))
    ms = jnp.concatenate(ms_blocks, axis=1) * (1.0 / CONV_GROUP_DIM)
    yc = (y * lax.rsqrt(ms + NORM_EPS) * gc_ref[...]).astype(BF16)
    mix = (jnp.dot(yc, w_ref[0:D_CONV, :], preferred_element_type=F32)
           + jnp.dot(ya_ref[...], w_ref[D_CONV:, :], preferred_element_type=F32))
    o_ref[...] = x_ref[...] + _rms_norm(mix, gp_ref[...], NORM_EPS)


def _out_proj(b, ch, ya, cw, gc, w, x2, gp, *, seq):
    t = x2.shape[0]
    n = t // ROW_TILE
    hb = ROW_TILE // CONV_HALO
    n_halo = t // CONV_HALO
    row = lambda i: (i, 0)
    const = lambda i: (0, 0)
    return pl.pallas_call(
        functools.partial(_out_proj_kernel, tiles_per_seq=seq // ROW_TILE),
        grid=(n,),
        in_specs=[
            pl.BlockSpec((ROW_TILE, D_CONV), row),
            pl.BlockSpec((ROW_TILE, D_CONV), row),
            pl.BlockSpec((CONV_HALO, D_CONV), lambda i: (jnp.maximum(i * hb - 1, 0), 0)),
            pl.BlockSpec((CONV_HALO, D_CONV), lambda i: (jnp.minimum((i + 1) * hb, n_halo - 1), 0)),
            pl.BlockSpec((ROW_TILE, D_ATT), row),
            pl.BlockSpec((3, D_CONV), const),
            pl.BlockSpec((1, D_CONV), const),
            pl.BlockSpec((D_MODEL, D_MODEL), const, pipeline_mode=pl.Buffered(1)),
            pl.BlockSpec((ROW_TILE, D_MODEL), row),
            pl.BlockSpec((1, D_MODEL), const),
        ],
        out_specs=pl.BlockSpec((ROW_TILE, D_MODEL), row),
        out_shape=jax.ShapeDtypeStruct((t, D_MODEL), F32),
        compiler_params=pltpu.CompilerParams(
            dimension_semantics=("arbitrary",), vmem_limit_bytes=VMEM_LIMIT),
        name="out_proj",
    )(b, ch, ch, ch, ya, cw, gc, w, x2, gp)


def _ffn_kernel(x_ref, xp_ref, xn_ref, g_ref, wu_ref, cw_ref, cb_ref, wd_ref, gp_ref, o_ref,
                hs_ref, act_ref, *, tiles_per_seq):
    i = pl.program_id(0)
    has_prev = (i % tiles_per_seq != 0).astype(F32)
    has_next = (i % tiles_per_seq != tiles_per_seq - 1).astype(F32)
    g = g_ref[...]
    x = x_ref[...]
    halo_row = lax.broadcasted_iota(jnp.int32, (FFN_HALO, D_MODEL), 0)
    h_prev = _rms_norm(xp_ref[...], g, NORM_EPS) * has_prev
    h_next = _rms_norm(xn_ref[...], g, NORM_EPS) * has_next
    hs_ref[0:FFN_HALO, :] = jnp.where(halo_row == FFN_HALO - 1, h_prev,
                                      jnp.where(halo_row == 0, h_next, 0.0)).astype(BF16)
    hs_ref[FFN_HALO:, :] = _rms_norm(x, g, NORM_EPS).astype(BF16)
    hs = hs_ref[...]
    ext = FFN_ROW_TILE + FFN_HALO

    def conv_chunk(c0):
        cols = slice(c0, c0 + FFN_CHUNK)
        u = jnp.dot(hs, wu_ref[:, cols], preferred_element_type=F32)
        cw = cw_ref[:, cols]
        mid = slice(FFN_HALO, FFN_HALO + FFN_ROW_TILE)
        return (pltpu.roll(u, 1, 0)[mid] * cw[0:1] + u[mid] * cw[1:2]
                + pltpu.roll(u, ext - 1, 0)[mid] * cw[2:3] + cb_ref[:, cols])

    for c in range(D_FF // FFN_CHUNK):
        gate = conv_chunk(c * FFN_CHUNK)
        up = conv_chunk(D_FF + c * FFN_CHUNK)
        act = gate * (1.0 / (1.0 + jnp.exp(-gate))) * up
        act_ref[:, c * FFN_CHUNK:(c + 1) * FFN_CHUNK] = act.astype(BF16)

    f = jnp.dot(act_ref[...], wd_ref[...], preferred_element_type=F32)
    o_ref[...] = x + _rms_norm(f, gp_ref[...], NORM_EPS)


def _ffn(x1, g, wu, cw, cb, wd, gp, *, seq):
    t = x1.shape[0]
    n = t // FFN_ROW_TILE
    hb = FFN_ROW_TILE // FFN_HALO
    n_halo = t // FFN_HALO
    row = lambda i: (i, 0)
    const = lambda i: (0, 0)
    return pl.pallas_call(
        functools.partial(_ffn_kernel, tiles_per_seq=seq // FFN_ROW_TILE),
        grid=(n,),
        in_specs=[
            pl.BlockSpec((FFN_ROW_TILE, D_MODEL), row),
            pl.BlockSpec((FFN_HALO, D_MODEL), lambda i: (jnp.maximum(i * hb - 1, 0), 0)),
            pl.BlockSpec((FFN_HALO, D_MODEL), lambda i: (jnp.minimum((i + 1) * hb, n_halo - 1), 0)),
            pl.BlockSpec((1, D_MODEL), const),
            pl.BlockSpec((D_MODEL, 2 * D_FF), const, pipeline_mode=pl.Buffered(1)),
            pl.BlockSpec((3, 2 * D_FF), const),
            pl.BlockSpec((1, 2 * D_FF), const),
            pl.BlockSpec((D_FF, D_MODEL), const, pipeline_mode=pl.Buffered(1)),
            pl.BlockSpec((1, D_MODEL), const),
        ],
        out_specs=pl.BlockSpec((FFN_ROW_TILE, D_MODEL), row),
        out_shape=jax.ShapeDtypeStruct((t, D_MODEL), F32),
        scratch_shapes=[
            pltpu.VMEM((FFN_ROW_TILE + FFN_HALO, D_MODEL), BF16),
            pltpu.VMEM((FFN_ROW_TILE, D_FF), BF16),
        ],
        compiler_params=pltpu.CompilerParams(
            dimension_semantics=("arbitrary",), vmem_limit_bytes=VMEM_LIMIT),
        name="conv_glu_ffn",
    )(x1, x1, x1, g, wu, cw, cb, wd, gp)


def kernel(x, g_mix_pre, w_mix_in, conv_w, g_conv_out, lambda_q1, lambda_k1, lambda_q2, lambda_k2,
           g_subln, w_mix_out, g_mix_post, g_ffn_pre, w_ffn_up, ffn_conv_w, ffn_conv_b, w_ffn_down,
           g_ffn_post):
    batch, seq, d = x.shape
    depth = g_mix_pre.shape[0]
    assert d == D_MODEL and seq % ROW_TILE == 0 and seq % Q_TILE == 0 and seq % K_TILE == 0
    assert seq % FFN_ROW_TILE == 0
    assert seq <= POS_SPLIT * POS_SPLIT
    t = batch * seq
    x2 = x.reshape(t, d)
    slopes = 2.0 ** (-8.0 * (jnp.arange(ATT_HEADS, dtype=F32) + 1.0) / ATT_HEADS)
    coef = _position_coefficients()

    for layer in range(depth):
        lambda_init = 0.8 - 0.6 * math.exp(-0.3 * layer)
        lam = jnp.stack([lambda_q1[layer], lambda_k1[layer], lambda_q2[layer], lambda_k2[layer]]).astype(F32)
        b, ch, ka, kb, qa, qb, vt = _in_proj(x2, g_mix_pre[layer][None], coef,
                                             w_mix_in[layer].astype(BF16), seq=seq)
        ya = _attention(slopes, lam, qa, qb, ka, kb, vt, g_subln[layer][:, None],
                        batch=batch, seq=seq, lambda_init=lambda_init)
        x2 = _out_proj(b, ch, ya, conv_w[layer], g_conv_out[layer][None],
                       w_mix_out[layer].astype(BF16), x2, g_mix_post[layer][None], seq=seq)
        x2 = _ffn(x2, g_ffn_pre[layer][None], w_ffn_up[layer].astype(BF16), ffn_conv_w[layer],
                  ffn_conv_b[layer][None], w_ffn_down[layer].astype(BF16), g_ffn_post[layer][None],
                  seq=seq)
    return x2.reshape(batch, seq, d)
```

```python
import functools
import math

import jax
import jax.numpy as jnp
import numpy as np
from jax import lax
from jax.experimental import pallas as pl
from jax.experimental.pallas import tpu as pltpu

F32 = jnp.float32
BF16 = jnp.bfloat16

D_MODEL = 1024
D_CONV = 512
D_ATT = 512
CONV_GROUP_DIM = 64
HEAD_DIM = 64
V_DIM = 2 * HEAD_DIM
SUM_ROWS = 16
V_AUG = V_DIM + SUM_ROWS
ATT_HEADS = D_ATT // V_DIM
D_MIX_IN = 3 * D_CONV + 3 * D_ATT
D_FF = 2816
NORM_EPS = 1e-6
SUBLN_EPS = 1e-5
QK_SCALE = HEAD_DIM ** -0.5
LOG2_E = math.log2(math.e)

ROW_TILE = 1024
Q_TILE = 512
K_TILE = 512
FFN_ROW_TILE = 1024
HALO = 16
FFN_CHUNK = 256
POS_SPLIT = 64
PIPELINE_BODY_STEPS = (6, 4, 2)
UNDERFLOW_LOG2 = 152.0
VMEM_LIMIT = 56 * 1024 * 1024


def _rms_norm(x, g, eps):
    return x * lax.rsqrt(jnp.mean(x * x, axis=-1, keepdims=True) + eps) * g


def _store_normed_with_halo(x, xp_ref, xn_ref, g, hs_ref, tiles_per_seq):
    i = pl.program_id(0)
    has_prev = (i % tiles_per_seq != 0).astype(F32)
    has_next = (i % tiles_per_seq != tiles_per_seq - 1).astype(F32)
    halo_row = lax.broadcasted_iota(jnp.int32, (HALO, D_MODEL), 0)
    h_prev = _rms_norm(xp_ref[...], g, NORM_EPS) * has_prev
    h_next = _rms_norm(xn_ref[...], g, NORM_EPS) * has_next
    hs_ref[0:HALO, :] = jnp.where(halo_row == HALO - 1, h_prev,
                                  jnp.where(halo_row == 0, h_next, 0.0)).astype(BF16)
    hs_ref[HALO:, :] = _rms_norm(x, g, NORM_EPS).astype(BF16)


def _conv3_rows(u, cw, rows):
    ext = HALO + rows
    mid = slice(HALO, ext)
    return (pltpu.roll(u, 1, 0)[mid] * cw[0:1] + u[mid] * cw[1:2]
            + pltpu.roll(u, ext - 1, 0)[mid] * cw[2:3])


def _in_proj_kernel(x_ref, xp_ref, xn_ref, g_ref, coef_ref, cw_ref, gc_ref, w_ref,
                    yc_ref, ka_ref, kb_ref, qa_ref, qb_ref, vt_ref, hs_ref, *, tiles_per_seq):
    _store_normed_with_halo(x_ref[...], xp_ref, xn_ref, g_ref[...], hs_ref, tiles_per_seq)
    h = hs_ref[HALO:, :]

    def proj(col, lhs=h):
        return jnp.dot(lhs, w_ref[:, col * D_CONV:(col + 1) * D_CONV], preferred_element_type=F32)

    h_ext = hs_ref[...]
    y = proj(0) * _conv3_rows(proj(1, h_ext) * proj(2, h_ext), cw_ref[...], ROW_TILE)
    z = y * y
    first_group = lax.broadcasted_iota(jnp.int32, (ROW_TILE, 2 * CONV_GROUP_DIM), 1) < CONV_GROUP_DIM
    ms_blocks = []
    for blk in range(D_CONV // (2 * CONV_GROUP_DIM)):
        zb = z[:, blk * 2 * CONV_GROUP_DIM:(blk + 1) * 2 * CONV_GROUP_DIM]
        s_first = jnp.sum(jnp.where(first_group, zb, 0.0), axis=1, keepdims=True)
        s_second = jnp.sum(jnp.where(first_group, 0.0, zb), axis=1, keepdims=True)
        ms_blocks.append(jnp.where(first_group, s_first, s_second))
    ms = jnp.concatenate(ms_blocks, axis=1) * (1.0 / CONV_GROUP_DIM)
    yc_ref[...] = (y * lax.rsqrt(ms + NORM_EPS) * gc_ref[...]).astype(BF16)

    pos = ((pl.program_id(0) % tiles_per_seq) * ROW_TILE
           + lax.broadcasted_iota(jnp.int32, (ROW_TILE, 1), 0))
    hi = (pos // POS_SPLIT).astype(F32)
    lo = (pos % POS_SPLIT).astype(F32)
    coef = coef_ref[...]
    k_pos = hi * coef[0:1] + lo * coef[1:2] + coef[2:3]
    q_pos = hi * coef[3:4] + lo * coef[4:5] + coef[5:6]
    lane = lax.broadcasted_iota(jnp.int32, (ROW_TILE, D_ATT), 1)
    first_map = (lane & HEAD_DIM) == 0
    q = proj(3) * (QK_SCALE * LOG2_E)
    qa_ref[...] = jnp.where(first_map, q, q_pos).T.astype(BF16)
    qb_ref[...] = jnp.where(first_map, q_pos, q).T.astype(BF16)
    k = proj(4)
    ka_ref[...] = jnp.where(first_map, k, k_pos).astype(BF16)
    kb_ref[...] = jnp.where(first_map, k_pos, k).astype(BF16)
    vt = proj(5).T.astype(BF16)
    ones_row = (lax.broadcasted_iota(jnp.int32, (SUM_ROWS, ROW_TILE), 0) == 0).astype(BF16)
    for hd in range(ATT_HEADS):
        vt_ref[hd * V_AUG:hd * V_AUG + V_DIM, :] = vt[hd * V_DIM:(hd + 1) * V_DIM]
        vt_ref[hd * V_AUG + V_DIM:(hd + 1) * V_AUG, :] = ones_row


def _position_coefficients():
    lane = jnp.arange(D_ATT)
    e = lane % HEAD_DIM
    slope = 2.0 ** (-8.0 * ((lane // V_DIM).astype(F32) + 1.0) / ATT_HEADS)
    pieces = []
    rest = np.float64(LOG2_E)
    for _ in range(3):
        piece = np.float64(rest.astype(BF16))
        pieces.append(piece)
        rest = rest - piece
    log2e_piece = jnp.asarray(pieces, F32)[e % 3]
    zero = jnp.zeros((D_ATT,), F32)
    rows = [
        jnp.where(e < 3, slope * POS_SPLIT, 0.0),
        jnp.where((e >= 3) & (e < 6), slope, 0.0),
        jnp.where((e >= 6) & (e < 12), log2e_piece, 0.0),
        jnp.where((e >= 6) & (e < 9), -slope * POS_SPLIT, 0.0),
        jnp.where((e >= 9) & (e < 12), -slope, 0.0),
        jnp.where(e < 6, log2e_piece, 0.0),
        zero, zero,
    ]
    return jnp.stack(rows).astype(F32)


def _halo_specs(tile, t):
    per_tile = tile // HALO
    last = t // HALO - 1
    return [pl.BlockSpec((HALO, D_MODEL), lambda i: (jnp.maximum(i * per_tile - 1, 0), 0)),
            pl.BlockSpec((HALO, D_MODEL), lambda i: (jnp.minimum((i + 1) * per_tile, last), 0))]


def _in_proj(x2, g, coef, cw, gc, w, *, seq):
    t = x2.shape[0]
    grid = (t // ROW_TILE,)
    row = lambda i: (i, 0)
    col = lambda i: (0, i)
    const = lambda i: (0, 0)
    return pl.pallas_call(
        functools.partial(_in_proj_kernel, tiles_per_seq=seq // ROW_TILE),
        grid=grid,
        in_specs=[
            pl.BlockSpec((ROW_TILE, D_MODEL), row),
            *_halo_specs(ROW_TILE, t),
            pl.BlockSpec((1, D_MODEL), const),
            pl.BlockSpec((8, D_ATT), const),
            pl.BlockSpec((3, D_CONV), const),
            pl.BlockSpec((1, D_CONV), const),
            pl.BlockSpec((D_MODEL, D_MIX_IN), const, pipeline_mode=pl.Buffered(1)),
        ],
        out_specs=[
            pl.BlockSpec((ROW_TILE, D_CONV), row),
            pl.BlockSpec((ROW_TILE, D_ATT), row),
            pl.BlockSpec((ROW_TILE, D_ATT), row),
            pl.BlockSpec((D_ATT, ROW_TILE), col),
            pl.BlockSpec((D_ATT, ROW_TILE), col),
            pl.BlockSpec((ATT_HEADS * V_AUG, ROW_TILE), col),
        ],
        out_shape=[
            jax.ShapeDtypeStruct((t, D_CONV), BF16),
            jax.ShapeDtypeStruct((t, D_ATT), BF16),
            jax.ShapeDtypeStruct((t, D_ATT), BF16),
            jax.ShapeDtypeStruct((D_ATT, t), BF16),
            jax.ShapeDtypeStruct((D_ATT, t), BF16),
            jax.ShapeDtypeStruct((ATT_HEADS * V_AUG, t), BF16),
        ],
        scratch_shapes=[pltpu.VMEM((HALO + ROW_TILE, D_MODEL), BF16)],
        compiler_params=pltpu.CompilerParams(
            dimension_semantics=("arbitrary",), vmem_limit_bytes=VMEM_LIMIT),
        name="in_proj",
    )(x2, x2, x2, g, coef, cw, gc, w)


def _attn_kernel(slopes_ref, lam_ref, qa_ref, qb_ref, ka_ref, kb_ref, vt_ref, g_ref, o_ref,
                 m_ref, acc_ref, s0_ref, s1_ref, smax0_ref, smax1_ref, corr_ref, reach_ref,
                 *, seq, lambda_init):
    head = pl.program_id(1)
    nk = seq // K_TILE
    decay = LOG2_E * slopes_ref[head]

    reach_ref[0] = nk - 1

    @pl.when(decay * (K_TILE * (nk - 2) + 1) > UNDERFLOW_LOG2)
    def _():
        def max_norm2(ref, feature_axis, first_half):
            x = ref[...].astype(F32)
            idx = lax.broadcasted_iota(jnp.int32, x.shape, feature_axis)
            feature = (idx < HEAD_DIM) if first_half else (idx >= HEAD_DIM)
            n2 = jnp.sum(jnp.where(feature, x * x, 0.0), axis=feature_axis, keepdims=True)
            return jnp.max(n2, axis=1 - feature_axis, keepdims=True)

        q2 = jnp.maximum(max_norm2(qa_ref, 0, True), max_norm2(qb_ref, 0, False))
        k2 = jnp.maximum(max_norm2(ka_ref, 1, True), max_norm2(kb_ref, 1, False))
        budget = 2.02 * jnp.sqrt(q2 * k2) + UNDERFLOW_LOG2
        reach = jnp.floor((budget / decay - 1.0) * (1.0 / K_TILE)) + 1.0
        reach = jnp.where(reach < nk - 1, jnp.maximum(reach, 0.0), nk - 1.0)
        reach_ref[0] = reach.astype(jnp.int32)[0, 0]

    reach = reach_ref[0]

    key_minus_query = (lax.broadcasted_iota(jnp.int32, (K_TILE, Q_TILE), 0)
                       - lax.broadcasted_iota(jnp.int32, (K_TILE, Q_TILE), 1))
    corr_ref[...] = (-2.0 * decay) * jnp.maximum(key_minus_query, 0).astype(F32)

    lv = lam_ref[...]
    lam = (jnp.exp(jnp.sum(lv[0:1] * lv[1:2], axis=-1, keepdims=True))
           - jnp.exp(jnp.sum(lv[2:3] * lv[3:4], axis=-1, keepdims=True)) + lambda_init)
    buf0 = (s0_ref, smax0_ref)
    buf1 = (s1_ref, smax1_ref)

    def store_scores(s, buf, mp):
        s_ref, smax_ref = buf
        s_ref[mp] = s
        smax_ref[mp] = jnp.max(s, axis=0, keepdims=True)

    def diagonal_scores(qi, buf):
        t0 = pl.multiple_of(qi * Q_TILE, Q_TILE)
        for mp, (k_ref, q_ref) in enumerate(((ka_ref, qa_ref), (kb_ref, qb_ref))):
            s = jnp.dot(k_ref[pl.ds(t0, K_TILE), :], q_ref[:, pl.ds(t0, Q_TILE)],
                        preferred_element_type=F32)
            store_scores(s + corr_ref[...], buf, mp)

    diagonal_scores(0, buf0)

    def query_tile(qi, carry):
        q0 = pl.multiple_of(qi * Q_TILE, Q_TILE)
        row = lax.broadcasted_iota(jnp.int32, (V_DIM, Q_TILE), 0)
        qa_before = qa_ref[:, pl.ds(q0, Q_TILE)]
        qb_before = qb_ref[:, pl.ds(q0, Q_TILE)]
        qa_after = jnp.where(row >= HEAD_DIM, -qa_before, qa_before)
        qb_after = jnp.where(row < HEAD_DIM, -qb_before, qb_before)

        m_ref[...] = jnp.full(m_ref.shape, -1e30, F32)
        acc_ref[...] = jnp.zeros(acc_ref.shape, F32)

        lo = jnp.maximum(qi - reach, 0)
        hi = jnp.minimum(qi + reach, nk - 1)
        odd = (hi - lo + 1) % 2
        grow_hi = jnp.where(hi < nk - 1, odd, 0)
        hi = hi + grow_hi
        lo = lo - (odd - grow_hi)
        visits = hi - lo + 1

        def tile_of(n):
            j = lo + n - 1
            return jnp.where(n == 0, qi, jnp.where(j < qi, j, j + 1))

        def scores(n, buf):
            j = tile_of(n)
            k0 = pl.multiple_of(j * K_TILE, K_TILE)
            keys_before = j < qi
            for mp, (k_ref, q_before, q_after) in enumerate(
                    ((ka_ref, qa_before, qa_after), (kb_ref, qb_before, qb_after))):
                qt = jnp.where(keys_before, q_before, q_after)
                s = jnp.dot(k_ref[pl.ds(k0, K_TILE), :], qt, preferred_element_type=F32)
                store_scores(s, buf, mp)

        def absorb(n, buf):
            s_ref, smax_ref = buf
            k0 = pl.multiple_of(tile_of(n) * K_TILE, K_TILE)
            vt = vt_ref[:, pl.ds(k0, K_TILE)]
            for mp in range(2):
                s = s_ref[mp]
                m_old = m_ref[mp]
                m_new = jnp.maximum(m_old, smax_ref[mp])
                alpha = jnp.exp2(m_old - m_new)
                p = jnp.exp2(s - m_new)
                acc_ref[mp] = alpha * acc_ref[mp] + jnp.dot(vt, p.astype(BF16),
                                                            preferred_element_type=F32)
                m_ref[mp] = m_new

        def pipeline_steps(first, count):
            for offset in range(0, count, 2):
                n = first + offset
                scores(n + 1, buf1)
                absorb(n, buf0)
                scores(n + 2, buf0)
                absorb(n + 1, buf1)

        done = jnp.int32(0)
        for body_steps in PIPELINE_BODY_STEPS:
            trips = (visits - 2 - done) // body_steps

            def body(p, c, start=done, body_steps=body_steps):
                pipeline_steps(start + body_steps * p, body_steps)
                return c

            lax.fori_loop(0, trips, body, 0)
            done = done + trips * body_steps
        scores(visits - 1, buf1)
        absorb(visits - 2, buf0)
        diagonal_scores(jnp.minimum(qi + 1, seq // Q_TILE - 1), buf0)
        absorb(visits - 1, buf1)

        def normalised(mp):
            a = acc_ref[mp]
            return a[:V_DIM] / a[V_DIM:V_DIM + 1]

        o = normalised(0) - lam * normalised(1)
        y = o * lax.rsqrt(jnp.mean(o * o, axis=0, keepdims=True) + SUBLN_EPS) * g_ref[...]
        o_ref[pl.ds(q0, Q_TILE), :] = (y * (1.0 - lambda_init)).T.astype(BF16)
        return carry

    lax.fori_loop(0, seq // Q_TILE, query_tile, 0)


def _attention(slopes, lam, qa, qb, ka, kb, vt, g_col, *, batch, seq, lambda_init):
    assert Q_TILE == K_TILE and (seq // K_TILE) % 2 == 0
    assert 8 % ATT_HEADS == 0
    t = batch * seq
    rows = lambda b, h: (b, h)
    cols = lambda b, h: (h, b)
    const = lambda b, h: (0, 0)
    return pl.pallas_call(
        functools.partial(_attn_kernel, seq=seq, lambda_init=lambda_init),
        grid=(batch, ATT_HEADS),
        in_specs=[
            pl.BlockSpec(memory_space=pltpu.SMEM),
            pl.BlockSpec((4, HEAD_DIM), const),
            pl.BlockSpec((V_DIM, seq), cols),
            pl.BlockSpec((V_DIM, seq), cols),
            pl.BlockSpec((seq, V_DIM), rows),
            pl.BlockSpec((seq, V_DIM), rows),
            pl.BlockSpec((V_AUG, seq), cols),
            pl.BlockSpec((V_DIM, 1), const),
        ],
        out_specs=pl.BlockSpec((seq, V_DIM), rows),
        out_shape=jax.ShapeDtypeStruct((t, D_ATT), BF16),
        scratch_shapes=[
            pltpu.VMEM((2, 1, Q_TILE), F32),
            pltpu.VMEM((2, V_AUG, Q_TILE), F32),
            pltpu.VMEM((2, K_TILE, Q_TILE), F32),
            pltpu.VMEM((2, K_TILE, Q_TILE), F32),
            pltpu.VMEM((2, 1, Q_TILE), F32),
            pltpu.VMEM((2, 1, Q_TILE), F32),
            pltpu.VMEM((K_TILE, Q_TILE), F32),
            pltpu.SMEM((1,), jnp.int32),
        ],
        compiler_params=pltpu.CompilerParams(
            dimension_semantics=("arbitrary", "arbitrary"), vmem_limit_bytes=VMEM_LIMIT),
        name="diff_attention",
    )(slopes, lam, qa, qb, ka, kb, vt, g_col)


def _out_proj_kernel(yc_ref, ya_ref, w_ref, x_ref, gp_ref, o_ref):
    mix = (jnp.dot(yc_ref[...], w_ref[0:D_CONV, :], preferred_element_type=F32)
           + jnp.dot(ya_ref[...], w_ref[D_CONV:, :], preferred_element_type=F32))
    o_ref[...] = x_ref[...] + _rms_norm(mix, gp_ref[...], NORM_EPS)


def _out_proj(yc, ya, w, x2, gp):
    t = x2.shape[0]
    row = lambda i: (i, 0)
    const = lambda i: (0, 0)
    return pl.pallas_call(
        _out_proj_kernel,
        grid=(t // ROW_TILE,),
        in_specs=[
            pl.BlockSpec((ROW_TILE, D_CONV), row),
            pl.BlockSpec((ROW_TILE, D_ATT), row),
            pl.BlockSpec((D_MODEL, D_MODEL), const, pipeline_mode=pl.Buffered(1)),
            pl.BlockSpec((ROW_TILE, D_MODEL), row),
            pl.BlockSpec((1, D_MODEL), const),
        ],
        out_specs=pl.BlockSpec((ROW_TILE, D_MODEL), row),
        out_shape=jax.ShapeDtypeStruct((t, D_MODEL), F32),
        compiler_params=pltpu.CompilerParams(
            dimension_semantics=("arbitrary",), vmem_limit_bytes=VMEM_LIMIT),
        name="out_proj",
    )(yc, ya, w, x2, gp)


def _ffn_kernel(x_ref, xp_ref, xn_ref, g_ref, wu_ref, cw_ref, cb_ref, wd_ref, gp_ref, o_ref,
                hs_ref, act_ref, *, tiles_per_seq):
    x = x_ref[...]
    _store_normed_with_halo(x, xp_ref, xn_ref, g_ref[...], hs_ref, tiles_per_seq)
    hs = hs_ref[...]

    def conv_chunk(c0):
        cols = slice(c0, c0 + FFN_CHUNK)
        u = jnp.dot(hs, wu_ref[:, cols], preferred_element_type=F32)
        return _conv3_rows(u, cw_ref[:, cols], FFN_ROW_TILE) + cb_ref[:, cols]

    for c in range(D_FF // FFN_CHUNK):
        gate = conv_chunk(c * FFN_CHUNK)
        up = conv_chunk(D_FF + c * FFN_CHUNK)
        act = gate * (1.0 / (1.0 + jnp.exp(-gate))) * up
        act_ref[:, c * FFN_CHUNK:(c + 1) * FFN_CHUNK] = act.astype(BF16)

    f = jnp.dot(act_ref[...], wd_ref[...], preferred_element_type=F32)
    o_ref[...] = x + _rms_norm(f, gp_ref[...], NORM_EPS)


def _ffn(x1, g, wu, cw, cb, wd, gp, *, seq):
    t = x1.shape[0]
    row = lambda i: (i, 0)
    const = lambda i: (0, 0)
    return pl.pallas_call(
        functools.partial(_ffn_kernel, tiles_per_seq=seq // FFN_ROW_TILE),
        grid=(t // FFN_ROW_TILE,),
        in_specs=[
            pl.BlockSpec((FFN_ROW_TILE, D_MODEL), row),
            *_halo_specs(FFN_ROW_TILE, t),
            pl.BlockSpec((1, D_MODEL), const),
            pl.BlockSpec((D_MODEL, 2 * D_FF), const, pipeline_mode=pl.Buffered(1)),
            pl.BlockSpec((3, 2 * D_FF), const),
            pl.BlockSpec((1, 2 * D_FF), const),
            pl.BlockSpec((D_FF, D_MODEL), const, pipeline_mode=pl.Buffered(1)),
            pl.BlockSpec((1, D_MODEL), const),
        ],
        out_specs=pl.BlockSpec((FFN_ROW_TILE, D_MODEL), row),
        out_shape=jax.ShapeDtypeStruct((t, D_MODEL), F32),
        scratch_shapes=[
            pltpu.VMEM((HALO + FFN_ROW_TILE, D_MODEL), BF16),
            pltpu.VMEM((FFN_ROW_TILE, D_FF), BF16),
        ],
        compiler_params=pltpu.CompilerParams(
            dimension_semantics=("arbitrary",), vmem_limit_bytes=VMEM_LIMIT),
        name="conv_glu_ffn",
    )(x1, x1, x1, g, wu, cw, cb, wd, gp)


def kernel(x, g_mix_pre, w_mix_in, conv_w, g_conv_out, lambda_q1, lambda_k1, lambda_q2, lambda_k2,
           g_subln, w_mix_out, g_mix_post, g_ffn_pre, w_ffn_up, ffn_conv_w, ffn_conv_b, w_ffn_down,
           g_ffn_post):
    batch, seq, d = x.shape
    depth = g_mix_pre.shape[0]
    assert d == D_MODEL and seq % ROW_TILE == 0 and seq % Q_TILE == 0 and seq % K_TILE == 0
    assert seq % FFN_ROW_TILE == 0
    assert seq <= POS_SPLIT * POS_SPLIT
    t = batch * seq
    x2 = x.reshape(t, d)
    slopes = 2.0 ** (-8.0 * (jnp.arange(ATT_HEADS, dtype=F32) + 1.0) / ATT_HEADS)
    coef = _position_coefficients()

    for layer in range(depth):
        lambda_init = 0.8 - 0.6 * math.exp(-0.3 * layer)
        lam = jnp.stack([lambda_q1[layer], lambda_k1[layer], lambda_q2[layer], lambda_k2[layer]]).astype(F32)
        yc, ka, kb, qa, qb, vt = _in_proj(x2, g_mix_pre[layer][None], coef, conv_w[layer],
                                          g_conv_out[layer][None], w_mix_in[layer].astype(BF16),
                                          seq=seq)
        ya = _attention(slopes, lam, qa, qb, ka, kb, vt, g_subln[layer][:, None],
                        batch=batch, seq=seq, lambda_init=lambda_init)
        x2 = _out_proj(yc, ya, w_mix_out[layer].astype(BF16), x2, g_mix_post[layer][None])
        x2 = _ffn(x2, g_ffn_pre[layer][None], w_ffn_up[layer].astype(BF16), ffn_conv_w[layer],
                  ffn_conv_b[layer][None], w_ffn_down[layer].astype(BF16), g_ffn_post[layer][None],
                  seq=seq)
    return x2.reshape(batch, seq, d)
```

```python
import functools
import math

import jax
import jax.numpy as jnp
import numpy as np
from jax import lax
from jax.experimental import pallas as pl
from jax.experimental.pallas import tpu as pltpu

F32 = jnp.float32
BF16 = jnp.bfloat16

D_MODEL = 1024
D_CONV = 512
D_ATT = 512
CONV_GROUP_DIM = 64
HEAD_DIM = 64
V_DIM = 2 * HEAD_DIM
SUM_ROWS = 16
V_AUG = V_DIM + SUM_ROWS
ATT_HEADS = D_ATT // V_DIM
D_MIX_IN = 3 * D_CONV + 3 * D_ATT
D_FF = 2816
NORM_EPS = 1e-6
SUBLN_EPS = 1e-5
QK_SCALE = HEAD_DIM ** -0.5
LOG2_E = math.log2(math.e)

ROW_TILE = 1024
Q_TILE = 512
K_TILE = 512
FFN_ROW_TILE = 1024
HALO = 16
FFN_CHUNK = 256
POS_SPLIT = 64
PIPELINE_BODY_STEPS = (6, 4, 2)
UNDERFLOW_LOG2 = 152.0
VMEM_LIMIT = 56 * 1024 * 1024


def _rms_norm(x, g, eps):
    return x * lax.rsqrt(jnp.mean(x * x, axis=-1, keepdims=True) + eps) * g


def _store_normed_with_halo(x, xp_ref, xn_ref, g, hs_ref, tiles_per_seq):
    i = pl.program_id(0)
    has_prev = (i % tiles_per_seq != 0).astype(F32)
    has_next = (i % tiles_per_seq != tiles_per_seq - 1).astype(F32)
    halo_row = lax.broadcasted_iota(jnp.int32, (HALO, D_MODEL), 0)
    h_prev = _rms_norm(xp_ref[...], g, NORM_EPS) * has_prev
    h_next = _rms_norm(xn_ref[...], g, NORM_EPS) * has_next
    hs_ref[0:HALO, :] = jnp.where(halo_row == HALO - 1, h_prev,
                                  jnp.where(halo_row == 0, h_next, 0.0)).astype(BF16)
    hs_ref[HALO:, :] = _rms_norm(x, g, NORM_EPS).astype(BF16)


def _conv3_rows(u, cw, rows):
    ext = HALO + rows
    mid = slice(HALO, ext)
    return (pltpu.roll(u, 1, 0)[mid] * cw[0:1] + u[mid] * cw[1:2]
            + pltpu.roll(u, ext - 1, 0)[mid] * cw[2:3])


def _in_proj_kernel(x_ref, xp_ref, xn_ref, g_ref, coef_ref, cw_ref, gc_ref, w_ref,
                    yc_ref, ka_ref, kb_ref, qa_ref, qb_ref, vt_ref, hs_ref, *, tiles_per_seq):
    _store_normed_with_halo(x_ref[...], xp_ref, xn_ref, g_ref[...], hs_ref, tiles_per_seq)
    h = hs_ref[HALO:, :]

    def proj(col, lhs=h):
        return jnp.dot(lhs, w_ref[:, col * D_CONV:(col + 1) * D_CONV], preferred_element_type=F32)

    h_ext = hs_ref[...]
    y = proj(0) * _conv3_rows(proj(1, h_ext) * proj(2, h_ext), cw_ref[...], ROW_TILE)
    z = y * y
    first_group = lax.broadcasted_iota(jnp.int32, (ROW_TILE, 2 * CONV_GROUP_DIM), 1) < CONV_GROUP_DIM
    ms_blocks = []
    for blk in range(D_CONV // (2 * CONV_GROUP_DIM)):
        zb = z[:, blk * 2 * CONV_GROUP_DIM:(blk + 1) * 2 * CONV_GROUP_DIM]
        s_first = jnp.sum(jnp.where(first_group, zb, 0.0), axis=1, keepdims=True)
        s_second = jnp.sum(jnp.where(first_group, 0.0, zb), axis=1, keepdims=True)
        ms_blocks.append(jnp.where(first_group, s_first, s_second))
    ms = jnp.concatenate(ms_blocks, axis=1) * (1.0 / CONV_GROUP_DIM)
    yc_ref[...] = (y * lax.rsqrt(ms + NORM_EPS) * gc_ref[...]).astype(BF16)

    pos = ((pl.program_id(0) % tiles_per_seq) * ROW_TILE
           + lax.broadcasted_iota(jnp.int32, (ROW_TILE, 1), 0))
    hi = (pos // POS_SPLIT).astype(F32)
    lo = (pos % POS_SPLIT).astype(F32)
    coef = coef_ref[...]
    k_pos = hi * coef[0:1] + lo * coef[1:2] + coef[2:3]
    q_pos = hi * coef[3:4] + lo * coef[4:5] + coef[5:6]
    lane = lax.broadcasted_iota(jnp.int32, (ROW_TILE, D_ATT), 1)
    first_map = (lane & HEAD_DIM) == 0
    q = proj(3) * (QK_SCALE * LOG2_E)
    qa_ref[...] = jnp.where(first_map, q, q_pos).T.astype(BF16)
    qb_ref[...] = jnp.where(first_map, q_pos, q).T.astype(BF16)
    k = proj(4)
    ka_ref[...] = jnp.where(first_map, k, k_pos).astype(BF16)
    kb_ref[...] = jnp.where(first_map, k_pos, k).astype(BF16)
    vt = proj(5).T.astype(BF16)
    ones_row = (lax.broadcasted_iota(jnp.int32, (SUM_ROWS, ROW_TILE), 0) == 0).astype(BF16)
    for hd in range(ATT_HEADS):
        vt_ref[hd * V_AUG:hd * V_AUG + V_DIM, :] = vt[hd * V_DIM:(hd + 1) * V_DIM]
        vt_ref[hd * V_AUG + V_DIM:(hd + 1) * V_AUG, :] = ones_row


def _position_coefficients():
    lane = jnp.arange(D_ATT)
    e = lane % HEAD_DIM
    slope = 2.0 ** (-8.0 * ((lane // V_DIM).astype(F32) + 1.0) / ATT_HEADS)
    pieces = []
    rest = np.float64(LOG2_E)
    for _ in range(3):
        piece = np.float64(rest.astype(BF16))
        pieces.append(piece)
        rest = rest - piece
    log2e_piece = jnp.asarray(pieces, F32)[e % 3]
    zero = jnp.zeros((D_ATT,), F32)
    rows = [
        jnp.where(e < 3, slope * POS_SPLIT, 0.0),
        jnp.where((e >= 3) & (e < 6), slope, 0.0),
        jnp.where((e >= 6) & (e < 12), log2e_piece, 0.0),
        jnp.where((e >= 6) & (e < 9), -slope * POS_SPLIT, 0.0),
        jnp.where((e >= 9) & (e < 12), -slope, 0.0),
        jnp.where(e < 6, log2e_piece, 0.0),
        zero, zero,
    ]
    return jnp.stack(rows).astype(F32)


def _halo_specs(tile, t):
    per_tile = tile // HALO
    last = t // HALO - 1
    return [pl.BlockSpec((HALO, D_MODEL), lambda i: (jnp.maximum(i * per_tile - 1, 0), 0)),
            pl.BlockSpec((HALO, D_MODEL), lambda i: (jnp.minimum((i + 1) * per_tile, last), 0))]


def _in_proj(x2, g, coef, cw, gc, w, *, seq):
    t = x2.shape[0]
    grid = (t // ROW_TILE,)
    row = lambda i: (i, 0)
    col = lambda i: (0, i)
    const = lambda i: (0, 0)
    return pl.pallas_call(
        functools.partial(_in_proj_kernel, tiles_per_seq=seq // ROW_TILE),
        grid=grid,
        in_specs=[
            pl.BlockSpec((ROW_TILE, D_MODEL), row),
            *_halo_specs(ROW_TILE, t),
            pl.BlockSpec((1, D_MODEL), const),
            pl.BlockSpec((8, D_ATT), const),
            pl.BlockSpec((3, D_CONV), const),
            pl.BlockSpec((1, D_CONV), const),
            pl.BlockSpec((D_MODEL, D_MIX_IN), const, pipeline_mode=pl.Buffered(1)),
        ],
        out_specs=[
            pl.BlockSpec((ROW_TILE, D_CONV), row),
            pl.BlockSpec((ROW_TILE, D_ATT), row),
            pl.BlockSpec((ROW_TILE, D_ATT), row),
            pl.BlockSpec((D_ATT, ROW_TILE), col),
            pl.BlockSpec((D_ATT, ROW_TILE), col),
            pl.BlockSpec((ATT_HEADS * V_AUG, ROW_TILE), col),
        ],
        out_shape=[
            jax.ShapeDtypeStruct((t, D_CONV), BF16),
            jax.ShapeDtypeStruct((t, D_ATT), BF16),
            jax.ShapeDtypeStruct((t, D_ATT), BF16),
            jax.ShapeDtypeStruct((D_ATT, t), BF16),
            jax.ShapeDtypeStruct((D_ATT, t), BF16),
            jax.ShapeDtypeStruct((ATT_HEADS * V_AUG, t), BF16),
        ],
        scratch_shapes=[pltpu.VMEM((HALO + ROW_TILE, D_MODEL), BF16)],
        compiler_params=pltpu.CompilerParams(
            dimension_semantics=("arbitrary",), vmem_limit_bytes=VMEM_LIMIT),
        name="in_proj",
    )(x2, x2, x2, g, coef, cw, gc, w)


def _attn_kernel(slopes_ref, lam_ref, qa_ref, qb_ref, ka_ref, kb_ref, vt_ref, g_ref, o_ref,
                 m_ref, acc_ref, s0_ref, s1_ref, smax0_ref, smax1_ref, corr_ref, diff_ref, reach_ref,
                 *, seq, lambda_init):
    head = pl.program_id(1)
    nk = seq // K_TILE
    decay = LOG2_E * slopes_ref[head]

    reach_ref[0] = nk - 1

    @pl.when(decay * (K_TILE * (nk - 2) + 1) > UNDERFLOW_LOG2)
    def _():
        def max_norm2(ref, feature_axis, first_half):
            x = ref[...].astype(F32)
            idx = lax.broadcasted_iota(jnp.int32, x.shape, feature_axis)
            feature = (idx < HEAD_DIM) if first_half else (idx >= HEAD_DIM)
            n2 = jnp.sum(jnp.where(feature, x * x, 0.0), axis=feature_axis, keepdims=True)
            return jnp.max(n2, axis=1 - feature_axis, keepdims=True)

        q2 = jnp.maximum(max_norm2(qa_ref, 0, True), max_norm2(qb_ref, 0, False))
        k2 = jnp.maximum(max_norm2(ka_ref, 1, True), max_norm2(kb_ref, 1, False))
        budget = 2.02 * jnp.sqrt(q2 * k2) + UNDERFLOW_LOG2
        reach = jnp.floor((budget / decay - 1.0) * (1.0 / K_TILE)) + 1.0
        reach = jnp.where(reach < nk - 1, jnp.maximum(reach, 0.0), nk - 1.0)
        reach_ref[0] = reach.astype(jnp.int32)[0, 0]

    reach = reach_ref[0]

    key_minus_query = (lax.broadcasted_iota(jnp.int32, (K_TILE, Q_TILE), 0)
                       - lax.broadcasted_iota(jnp.int32, (K_TILE, Q_TILE), 1))
    corr_ref[...] = (-2.0 * decay) * jnp.maximum(key_minus_query, 0).astype(F32)

    lv = lam_ref[...]
    lam = (jnp.exp(jnp.sum(lv[0:1] * lv[1:2], axis=-1, keepdims=True))
           - jnp.exp(jnp.sum(lv[2:3] * lv[3:4], axis=-1, keepdims=True)) + lambda_init)
    buf0 = (s0_ref, smax0_ref)
    buf1 = (s1_ref, smax1_ref)

    def store_scores(s, buf, mp):
        s_ref, smax_ref = buf
        s_ref[mp] = s
        smax_ref[mp] = jnp.max(s, axis=0, keepdims=True)

    def diagonal_scores(qi, buf):
        t0 = pl.multiple_of(qi * Q_TILE, Q_TILE)
        for mp, (k_ref, q_ref) in enumerate(((ka_ref, qa_ref), (kb_ref, qb_ref))):
            s = jnp.dot(k_ref[pl.ds(t0, K_TILE), :], q_ref[:, pl.ds(t0, Q_TILE)],
                        preferred_element_type=F32)
            store_scores(s + corr_ref[...], buf, mp)

    diagonal_scores(0, buf0)

    def query_tile(qi, carry):
        q0 = pl.multiple_of(qi * Q_TILE, Q_TILE)
        row = lax.broadcasted_iota(jnp.int32, (V_DIM, Q_TILE), 0)
        qa_before = qa_ref[:, pl.ds(q0, Q_TILE)]
        qb_before = qb_ref[:, pl.ds(q0, Q_TILE)]
        qa_after = jnp.where(row >= HEAD_DIM, -qa_before, qa_before)
        qb_after = jnp.where(row < HEAD_DIM, -qb_before, qb_before)

        m_ref[...] = jnp.full(m_ref.shape, -1e30, F32)
        acc_ref[...] = jnp.zeros(acc_ref.shape, F32)

        lo = jnp.maximum(qi - reach, 0)
        hi = jnp.minimum(qi + reach, nk - 1)
        odd = (hi - lo + 1) % 2
        grow_hi = jnp.where(hi < nk - 1, odd, 0)
        hi = hi + grow_hi
        lo = lo - (odd - grow_hi)
        visits = hi - lo + 1

        def tile_of(n):
            j = lo + n - 1
            return jnp.where(n == 0, qi, jnp.where(j < qi, j, j + 1))

        def scores(n, buf):
            j = tile_of(n)
            k0 = pl.multiple_of(j * K_TILE, K_TILE)
            keys_before = j < qi
            for mp, (k_ref, q_before, q_after) in enumerate(
                    ((ka_ref, qa_before, qa_after), (kb_ref, qb_before, qb_after))):
                qt = jnp.where(keys_before, q_before, q_after)
                s = jnp.dot(k_ref[pl.ds(k0, K_TILE), :], qt, preferred_element_type=F32)
                store_scores(s, buf, mp)

        def absorb(n, buf):
            s_ref, smax_ref = buf
            k0 = pl.multiple_of(tile_of(n) * K_TILE, K_TILE)
            vt = vt_ref[:, pl.ds(k0, K_TILE)]
            for mp in range(2):
                s = s_ref[mp]
                m_old = m_ref[mp]
                m_new = jnp.maximum(m_old, smax_ref[mp])
                alpha = jnp.exp2(m_old - m_new)
                p = jnp.exp2(s - m_new)
                acc_ref[mp] = alpha * acc_ref[mp] + jnp.dot(vt, p.astype(BF16),
                                                            preferred_element_type=F32)
                m_ref[mp] = m_new

        def pipeline_steps(first, count):
            for offset in range(0, count, 2):
                n = first + offset
                scores(n + 1, buf1)
                absorb(n, buf0)
                scores(n + 2, buf0)
                absorb(n + 1, buf1)

        done = jnp.int32(0)
        for body_steps in PIPELINE_BODY_STEPS:
            trips = (visits - 2 - done) // body_steps

            def body(p, c, start=done, body_steps=body_steps):
                pipeline_steps(start + body_steps * p, body_steps)
                return c

            lax.fori_loop(0, trips, body, 0)
            done = done + trips * body_steps
        scores(visits - 1, buf1)
        absorb(visits - 2, buf0)
        diagonal_scores(jnp.minimum(qi + 1, seq // Q_TILE - 1), buf0)
        sub_ln_store(jnp.maximum(qi - 1, 0))
        absorb(visits - 1, buf1)

        def normalised(mp):
            a = acc_ref[mp]
            return a[:V_DIM] / a[V_DIM:V_DIM + 1]

        diff_ref[...] = normalised(0) - lam * normalised(1)
        return carry

    def sub_ln_store(qi):
        o = diff_ref[...]
        y = o * lax.rsqrt(jnp.mean(o * o, axis=0, keepdims=True) + SUBLN_EPS) * g_ref[...]
        start = qi * Q_TILE
        if not isinstance(start, int):
            start = pl.multiple_of(start, Q_TILE)
        o_ref[pl.ds(start, Q_TILE), :] = (y * (1.0 - lambda_init)).T.astype(BF16)

    diff_ref[...] = jnp.zeros(diff_ref.shape, F32)
    lax.fori_loop(0, seq // Q_TILE, query_tile, 0)
    sub_ln_store(seq // Q_TILE - 1)


def _attention(slopes, lam, qa, qb, ka, kb, vt, g_col, *, batch, seq, lambda_init):
    assert Q_TILE == K_TILE and (seq // K_TILE) % 2 == 0
    assert 8 % ATT_HEADS == 0
    t = batch * seq
    rows = lambda b, h: (b, h)
    cols = lambda b, h: (h, b)
    const = lambda b, h: (0, 0)
    return pl.pallas_call(
        functools.partial(_attn_kernel, seq=seq, lambda_init=lambda_init),
        grid=(batch, ATT_HEADS),
        in_specs=[
            pl.BlockSpec(memory_space=pltpu.SMEM),
            pl.BlockSpec((4, HEAD_DIM), const),
            pl.BlockSpec((V_DIM, seq), cols),
            pl.BlockSpec((V_DIM, seq), cols),
            pl.BlockSpec((seq, V_DIM), rows),
            pl.BlockSpec((seq, V_DIM), rows),
            pl.BlockSpec((V_AUG, seq), cols),
            pl.BlockSpec((V_DIM, 1), const),
        ],
        out_specs=pl.BlockSpec((seq, V_DIM), rows),
        out_shape=jax.ShapeDtypeStruct((t, D_ATT), BF16),
        scratch_shapes=[
            pltpu.VMEM((2, 1, Q_TILE), F32),
            pltpu.VMEM((2, V_AUG, Q_TILE), F32),
            pltpu.VMEM((2, K_TILE, Q_TILE), F32),
            pltpu.VMEM((2, K_TILE, Q_TILE), F32),
            pltpu.VMEM((2, 1, Q_TILE), F32),
            pltpu.VMEM((2, 1, Q_TILE), F32),
            pltpu.VMEM((K_TILE, Q_TILE), F32),
            pltpu.VMEM((V_DIM, Q_TILE), F32),
            pltpu.SMEM((1,), jnp.int32),
        ],
        compiler_params=pltpu.CompilerParams(
            dimension_semantics=("arbitrary", "arbitrary"), vmem_limit_bytes=VMEM_LIMIT),
        name="diff_attention",
    )(slopes, lam, qa, qb, ka, kb, vt, g_col)


def _out_proj_kernel(yc_ref, ya_ref, w_ref, x_ref, gp_ref, o_ref):
    mix = (jnp.dot(yc_ref[...], w_ref[0:D_CONV, :], preferred_element_type=F32)
           + jnp.dot(ya_ref[...], w_ref[D_CONV:, :], preferred_element_type=F32))
    o_ref[...] = x_ref[...] + _rms_norm(mix, gp_ref[...], NORM_EPS)


def _out_proj(yc, ya, w, x2, gp):
    t = x2.shape[0]
    row = lambda i: (i, 0)
    const = lambda i: (0, 0)
    return pl.pallas_call(
        _out_proj_kernel,
        grid=(t // ROW_TILE,),
        in_specs=[
            pl.BlockSpec((ROW_TILE, D_CONV), row),
            pl.BlockSpec((ROW_TILE, D_ATT), row),
            pl.BlockSpec((D_MODEL, D_MODEL), const, pipeline_mode=pl.Buffered(1)),
            pl.BlockSpec((ROW_TILE, D_MODEL), row),
            pl.BlockSpec((1, D_MODEL), const),
        ],
        out_specs=pl.BlockSpec((ROW_TILE, D_MODEL), row),
        out_shape=jax.ShapeDtypeStruct((t, D_MODEL), F32),
        compiler_params=pltpu.CompilerParams(
            dimension_semantics=("arbitrary",), vmem_limit_bytes=VMEM_LIMIT),
        name="out_proj",
    )(yc, ya, w, x2, gp)


def _ffn_kernel(x_ref, xp_ref, xn_ref, g_ref, wu_ref, cw_ref, cb_ref, wd_ref, gp_ref, o_ref,
                hs_ref, act_ref, *, tiles_per_seq):
    x = x_ref[...]
    _store_normed_with_halo(x, xp_ref, xn_ref, g_ref[...], hs_ref, tiles_per_seq)
    hs = hs_ref[...]

    def conv_chunk(c0):
        cols = slice(c0, c0 + FFN_CHUNK)
        u = jnp.dot(hs, wu_ref[:, cols], preferred_element_type=F32)
        return _conv3_rows(u, cw_ref[:, cols], FFN_ROW_TILE) + cb_ref[:, cols]

    for c in range(D_FF // FFN_CHUNK):
        gate = conv_chunk(c * FFN_CHUNK)
        up = conv_chunk(D_FF + c * FFN_CHUNK)
        act = gate * (1.0 / (1.0 + jnp.exp(-gate))) * up
        act_ref[:, c * FFN_CHUNK:(c + 1) * FFN_CHUNK] = act.astype(BF16)

    f = jnp.dot(act_ref[...], wd_ref[...], preferred_element_type=F32)
    o_ref[...] = x + _rms_norm(f, gp_ref[...], NORM_EPS)


def _ffn(x1, g, wu, cw, cb, wd, gp, *, seq):
    t = x1.shape[0]
    row = lambda i: (i, 0)
    const = lambda i: (0, 0)
    return pl.pallas_call(
        functools.partial(_ffn_kernel, tiles_per_seq=seq // FFN_ROW_TILE),
        grid=(t // FFN_ROW_TILE,),
        in_specs=[
            pl.BlockSpec((FFN_ROW_TILE, D_MODEL), row),
            *_halo_specs(FFN_ROW_TILE, t),
            pl.BlockSpec((1, D_MODEL), const),
            pl.BlockSpec((D_MODEL, 2 * D_FF), const, pipeline_mode=pl.Buffered(1)),
            pl.BlockSpec((3, 2 * D_FF), const),
            pl.BlockSpec((1, 2 * D_FF), const),
            pl.BlockSpec((D_FF, D_MODEL), const, pipeline_mode=pl.Buffered(1)),
            pl.BlockSpec((1, D_MODEL), const),
        ],
        out_specs=pl.BlockSpec((FFN_ROW_TILE, D_MODEL), row),
        out_shape=jax.ShapeDtypeStruct((t, D_MODEL), F32),
        scratch_shapes=[
            pltpu.VMEM((HALO + FFN_ROW_TILE, D_MODEL), BF16),
            pltpu.VMEM((FFN_ROW_TILE, D_FF), BF16),
        ],
        compiler_params=pltpu.CompilerParams(
            dimension_semantics=("arbitrary",), vmem_limit_bytes=VMEM_LIMIT),
        name="conv_glu_ffn",
    )(x1, x1, x1, g, wu, cw, cb, wd, gp)


def kernel(x, g_mix_pre, w_mix_in, conv_w, g_conv_out, lambda_q1, lambda_k1, lambda_q2, lambda_k2,
           g_subln, w_mix_out, g_mix_post, g_ffn_pre, w_ffn_up, ffn_conv_w, ffn_conv_b, w_ffn_down,
           g_ffn_post):
    batch, seq, d = x.shape
    depth = g_mix_pre.shape[0]
    assert d == D_MODEL and seq % ROW_TILE == 0 and seq % Q_TILE == 0 and seq % K_TILE == 0
    assert seq % FFN_ROW_TILE == 0
    assert seq <= POS_SPLIT * POS_SPLIT
    t = batch * seq
    x2 = x.reshape(t, d)
    slopes = 2.0 ** (-8.0 * (jnp.arange(ATT_HEADS, dtype=F32) + 1.0) / ATT_HEADS)
    coef = _position_coefficients()

    for layer in range(depth):
        lambda_init = 0.8 - 0.6 * math.exp(-0.3 * layer)
        lam = jnp.stack([lambda_q1[layer], lambda_k1[layer], lambda_q2[layer], lambda_k2[layer]]).astype(F32)
        yc, ka, kb, qa, qb, vt = _in_proj(x2, g_mix_pre[layer][None], coef, conv_w[layer],
                                          g_conv_out[layer][None], w_mix_in[layer].astype(BF16),
                                          seq=seq)
        ya = _attention(slopes, lam, qa, qb, ka, kb, vt, g_subln[layer][:, None],
                        batch=batch, seq=seq, lambda_init=lambda_init)
        x2 = _out_proj(yc, ya, w_mix_out[layer].astype(BF16), x2, g_mix_post[layer][None])
        x2 = _ffn(x2, g_ffn_pre[layer][None], w_ffn_up[layer].astype(BF16), ffn_conv_w[layer],
                  ffn_conv_b[layer][None], w_ffn_down[layer].astype(BF16), g_ffn_post[layer][None],
                  seq=seq)
    return x2.reshape(batch, seq, d)
```

```python
import functools
import math

import jax
import jax.numpy as jnp
import numpy as np
from jax import lax
from jax.experimental import pallas as pl
from jax.experimental.pallas import tpu as pltpu

F32 = jnp.float32
BF16 = jnp.bfloat16

D_MODEL = 1024
D_CONV = 512
D_ATT = 512
CONV_GROUP_DIM = 64
HEAD_DIM = 64
V_DIM = 2 * HEAD_DIM
SUM_ROWS = 16
V_AUG = V_DIM + SUM_ROWS
ATT_HEADS = D_ATT // V_DIM
D_MIX_IN = 3 * D_CONV + 3 * D_ATT
D_FF = 2816
NORM_EPS = 1e-6
SUBLN_EPS = 1e-5
QK_SCALE = HEAD_DIM ** -0.5
LOG2_E = math.log2(math.e)

ROW_TILE = 1024
Q_TILE = 512
K_TILE = 512
FFN_ROW_TILE = 1024
HALO = 16
FFN_CHUNK = 256
POS_SPLIT = 64
PIPELINE_BODY_STEPS = (6, 4, 2)
UNDERFLOW_LOG2 = 152.0
VMEM_LIMIT = 56 * 1024 * 1024
MIX_FFN_VMEM_LIMIT = 62 * 1024 * 1024


def _rms_norm(x, g, eps):
    return x * lax.rsqrt(jnp.mean(x * x, axis=-1, keepdims=True) + eps) * g


def _store_normed_with_halo(x, xp_ref, xn_ref, g, hs_ref, tiles_per_seq):
    i = pl.program_id(0)
    has_prev = (i % tiles_per_seq != 0).astype(F32)
    has_next = (i % tiles_per_seq != tiles_per_seq - 1).astype(F32)
    halo_row = lax.broadcasted_iota(jnp.int32, (HALO, D_MODEL), 0)
    h_prev = _rms_norm(xp_ref[...], g, NORM_EPS) * has_prev
    h_next = _rms_norm(xn_ref[...], g, NORM_EPS) * has_next
    hs_ref[0:HALO, :] = jnp.where(halo_row == HALO - 1, h_prev,
                                  jnp.where(halo_row == 0, h_next, 0.0)).astype(BF16)
    hs_ref[HALO:, :] = _rms_norm(x, g, NORM_EPS).astype(BF16)


def _conv3_rows(u, cw, rows):
    ext = HALO + rows
    mid = slice(HALO, ext)
    return (pltpu.roll(u, 1, 0)[mid] * cw[0:1] + u[mid] * cw[1:2]
            + pltpu.roll(u, ext - 1, 0)[mid] * cw[2:3])


def _in_proj_kernel(x_ref, xp_ref, xn_ref, g_ref, coef_ref, cw_ref, gc_ref, w_ref,
                    yc_ref, ka_ref, kb_ref, qa_ref, qb_ref, vt_ref, hs_ref, *, tiles_per_seq):
    _store_normed_with_halo(x_ref[...], xp_ref, xn_ref, g_ref[...], hs_ref, tiles_per_seq)
    h = hs_ref[HALO:, :]

    def proj(col, lhs=h):
        return jnp.dot(lhs, w_ref[:, col * D_CONV:(col + 1) * D_CONV], preferred_element_type=F32)

    h_ext = hs_ref[...]
    y = proj(0) * _conv3_rows(proj(1, h_ext) * proj(2, h_ext), cw_ref[...], ROW_TILE)
    z = y * y
    first_group = lax.broadcasted_iota(jnp.int32, (ROW_TILE, 2 * CONV_GROUP_DIM), 1) < CONV_GROUP_DIM
    ms_blocks = []
    for blk in range(D_CONV // (2 * CONV_GROUP_DIM)):
        zb = z[:, blk * 2 * CONV_GROUP_DIM:(blk + 1) * 2 * CONV_GROUP_DIM]
        s_first = jnp.sum(jnp.where(first_group, zb, 0.0), axis=1, keepdims=True)
        s_second = jnp.sum(jnp.where(first_group, 0.0, zb), axis=1, keepdims=True)
        ms_blocks.append(jnp.where(first_group, s_first, s_second))
    ms = jnp.concatenate(ms_blocks, axis=1) * (1.0 / CONV_GROUP_DIM)
    yc_ref[...] = (y * lax.rsqrt(ms + NORM_EPS) * gc_ref[...]).astype(BF16)

    pos = ((pl.program_id(0) % tiles_per_seq) * ROW_TILE
           + lax.broadcasted_iota(jnp.int32, (ROW_TILE, 1), 0))
    hi = (pos // POS_SPLIT).astype(F32)
    lo = (pos % POS_SPLIT).astype(F32)
    coef = coef_ref[...]
    k_pos = hi * coef[0:1] + lo * coef[1:2] + coef[2:3]
    q_pos = hi * coef[3:4] + lo * coef[4:5] + coef[5:6]
    lane = lax.broadcasted_iota(jnp.int32, (ROW_TILE, D_ATT), 1)
    first_map = (lane & HEAD_DIM) == 0
    q = proj(3) * (QK_SCALE * LOG2_E)
    qa_ref[...] = jnp.where(first_map, q, q_pos).T.astype(BF16)
    qb_ref[...] = jnp.where(first_map, q_pos, q).T.astype(BF16)
    k = proj(4)
    ka_ref[...] = jnp.where(first_map, k, k_pos).astype(BF16)
    kb_ref[...] = jnp.where(first_map, k_pos, k).astype(BF16)
    vt = proj(5).T.astype(BF16)
    ones_row = (lax.broadcasted_iota(jnp.int32, (SUM_ROWS, ROW_TILE), 0) == 0).astype(BF16)
    for hd in range(ATT_HEADS):
        vt_ref[hd * V_AUG:hd * V_AUG + V_DIM, :] = vt[hd * V_DIM:(hd + 1) * V_DIM]
        vt_ref[hd * V_AUG + V_DIM:(hd + 1) * V_AUG, :] = ones_row


def _position_coefficients():
    lane = jnp.arange(D_ATT)
    e = lane % HEAD_DIM
    slope = 2.0 ** (-8.0 * ((lane // V_DIM).astype(F32) + 1.0) / ATT_HEADS)
    pieces = []
    rest = np.float64(LOG2_E)
    for _ in range(3):
        piece = np.float64(rest.astype(BF16))
        pieces.append(piece)
        rest = rest - piece
    log2e_piece = jnp.asarray(pieces, F32)[e % 3]
    zero = jnp.zeros((D_ATT,), F32)
    rows = [
        jnp.where(e < 3, slope * POS_SPLIT, 0.0),
        jnp.where((e >= 3) & (e < 6), slope, 0.0),
        jnp.where((e >= 6) & (e < 12), log2e_piece, 0.0),
        jnp.where((e >= 6) & (e < 9), -slope * POS_SPLIT, 0.0),
        jnp.where((e >= 9) & (e < 12), -slope, 0.0),
        jnp.where(e < 6, log2e_piece, 0.0),
        zero, zero,
    ]
    return jnp.stack(rows).astype(F32)


def _halo_specs(tile, t, width=D_MODEL):
    per_tile = tile // HALO
    last = t // HALO - 1
    return [pl.BlockSpec((HALO, width), lambda i: (jnp.maximum(i * per_tile - 1, 0), 0)),
            pl.BlockSpec((HALO, width), lambda i: (jnp.minimum((i + 1) * per_tile, last), 0))]


def _in_proj(x2, g, coef, cw, gc, w, *, seq):
    t = x2.shape[0]
    grid = (t // ROW_TILE,)
    row = lambda i: (i, 0)
    col = lambda i: (0, i)
    const = lambda i: (0, 0)
    return pl.pallas_call(
        functools.partial(_in_proj_kernel, tiles_per_seq=seq // ROW_TILE),
        grid=grid,
        in_specs=[
            pl.BlockSpec((ROW_TILE, D_MODEL), row),
            *_halo_specs(ROW_TILE, t),
            pl.BlockSpec((1, D_MODEL), const),
            pl.BlockSpec((8, D_ATT), const),
            pl.BlockSpec((3, D_CONV), const),
            pl.BlockSpec((1, D_CONV), const),
            pl.BlockSpec((D_MODEL, D_MIX_IN), const, pipeline_mode=pl.Buffered(1)),
        ],
        out_specs=[
            pl.BlockSpec((ROW_TILE, D_CONV), row),
            pl.BlockSpec((ROW_TILE, D_ATT), row),
            pl.BlockSpec((ROW_TILE, D_ATT), row),
            pl.BlockSpec((D_ATT, ROW_TILE), col),
            pl.BlockSpec((D_ATT, ROW_TILE), col),
            pl.BlockSpec((ATT_HEADS * V_AUG, ROW_TILE), col),
        ],
        out_shape=[
            jax.ShapeDtypeStruct((t, D_CONV), BF16),
            jax.ShapeDtypeStruct((t, D_ATT), BF16),
            jax.ShapeDtypeStruct((t, D_ATT), BF16),
            jax.ShapeDtypeStruct((D_ATT, t), BF16),
            jax.ShapeDtypeStruct((D_ATT, t), BF16),
            jax.ShapeDtypeStruct((ATT_HEADS * V_AUG, t), BF16),
        ],
        scratch_shapes=[pltpu.VMEM((HALO + ROW_TILE, D_MODEL), BF16)],
        compiler_params=pltpu.CompilerParams(
            dimension_semantics=("arbitrary",), vmem_limit_bytes=VMEM_LIMIT),
        name="in_proj",
    )(x2, x2, x2, g, coef, cw, gc, w)


def _attn_kernel(slopes_ref, lam_ref, qa_ref, qb_ref, ka_ref, kb_ref, vt_ref, g_ref, o_ref,
                 m_ref, acc_ref, s0_ref, s1_ref, smax0_ref, smax1_ref, corr_ref, diff_ref, reach_ref,
                 *, seq, lambda_init):
    head = pl.program_id(1)
    nk = seq // K_TILE
    decay = LOG2_E * slopes_ref[head]

    reach_ref[0] = nk - 1

    @pl.when(decay * (K_TILE * (nk - 2) + 1) > UNDERFLOW_LOG2)
    def _():
        def max_norm2(ref, feature_axis, first_half):
            x = ref[...].astype(F32)
            idx = lax.broadcasted_iota(jnp.int32, x.shape, feature_axis)
            feature = (idx < HEAD_DIM) if first_half else (idx >= HEAD_DIM)
            n2 = jnp.sum(jnp.where(feature, x * x, 0.0), axis=feature_axis, keepdims=True)
            return jnp.max(n2, axis=1 - feature_axis, keepdims=True)

        q2 = jnp.maximum(max_norm2(qa_ref, 0, True), max_norm2(qb_ref, 0, False))
        k2 = jnp.maximum(max_norm2(ka_ref, 1, True), max_norm2(kb_ref, 1, False))
        budget = 2.02 * jnp.sqrt(q2 * k2) + UNDERFLOW_LOG2
        reach = jnp.floor((budget / decay - 1.0) * (1.0 / K_TILE)) + 1.0
        reach = jnp.where(reach < nk - 1, jnp.maximum(reach, 0.0), nk - 1.0)
        reach_ref[0] = reach.astype(jnp.int32)[0, 0]

    reach = reach_ref[0]

    key_minus_query = (lax.broadcasted_iota(jnp.int32, (K_TILE, Q_TILE), 0)
                       - lax.broadcasted_iota(jnp.int32, (K_TILE, Q_TILE), 1))
    corr_ref[...] = (-2.0 * decay) * jnp.maximum(key_minus_query, 0).astype(F32)

    lv = lam_ref[...]
    lam = (jnp.exp(jnp.sum(lv[0:1] * lv[1:2], axis=-1, keepdims=True))
           - jnp.exp(jnp.sum(lv[2:3] * lv[3:4], axis=-1, keepdims=True)) + lambda_init)
    buf0 = (s0_ref, smax0_ref)
    buf1 = (s1_ref, smax1_ref)

    def store_scores(s, buf, mp):
        s_ref, smax_ref = buf
        s_ref[mp] = s
        smax_ref[mp] = jnp.max(s, axis=0, keepdims=True)

    def diagonal_scores(qi, buf):
        t0 = pl.multiple_of(qi * Q_TILE, Q_TILE)
        for mp, (k_ref, q_ref) in enumerate(((ka_ref, qa_ref), (kb_ref, qb_ref))):
            s = jnp.dot(k_ref[pl.ds(t0, K_TILE), :], q_ref[:, pl.ds(t0, Q_TILE)],
                        preferred_element_type=F32)
            store_scores(s + corr_ref[...], buf, mp)

    diagonal_scores(0, buf0)

    def query_tile(qi, carry):
        q0 = pl.multiple_of(qi * Q_TILE, Q_TILE)
        row = lax.broadcasted_iota(jnp.int32, (V_DIM, Q_TILE), 0)
        qa_before = qa_ref[:, pl.ds(q0, Q_TILE)]
        qb_before = qb_ref[:, pl.ds(q0, Q_TILE)]
        qa_after = jnp.where(row >= HEAD_DIM, -qa_before, qa_before)
        qb_after = jnp.where(row < HEAD_DIM, -qb_before, qb_before)

        m_ref[...] = jnp.full(m_ref.shape, -1e30, F32)
        acc_ref[...] = jnp.zeros(acc_ref.shape, F32)

        lo = jnp.maximum(qi - reach, 0)
        hi = jnp.minimum(qi + reach, nk - 1)
        odd = (hi - lo + 1) % 2
        grow_hi = jnp.where(hi < nk - 1, odd, 0)
        hi = hi + grow_hi
        lo = lo - (odd - grow_hi)
        visits = hi - lo + 1

        def tile_of(n):
            j = lo + n - 1
            return jnp.where(n == 0, qi, jnp.where(j < qi, j, j + 1))

        def scores(n, buf):
            j = tile_of(n)
            k0 = pl.multiple_of(j * K_TILE, K_TILE)
            keys_before = j < qi
            for mp, (k_ref, q_before, q_after) in enumerate(
                    ((ka_ref, qa_before, qa_after), (kb_ref, qb_before, qb_after))):
                qt = jnp.where(keys_before, q_before, q_after)
                s = jnp.dot(k_ref[pl.ds(k0, K_TILE), :], qt, preferred_element_type=F32)
                store_scores(s, buf, mp)

        def absorb(n, buf):
            s_ref, smax_ref = buf
            k0 = pl.multiple_of(tile_of(n) * K_TILE, K_TILE)
            vt = vt_ref[:, pl.ds(k0, K_TILE)]
            for mp in range(2):
                s = s_ref[mp]
                m_old = m_ref[mp]
                m_new = jnp.maximum(m_old, smax_ref[mp])
                alpha = jnp.exp2(m_old - m_new)
                p = jnp.exp2(s - m_new)
                acc_ref[mp] = alpha * acc_ref[mp] + jnp.dot(vt, p.astype(BF16),
                                                            preferred_element_type=F32)
                m_ref[mp] = m_new

        def pipeline_steps(first, count):
            for offset in range(0, count, 2):
                n = first + offset
                scores(n + 1, buf1)
                absorb(n, buf0)
                scores(n + 2, buf0)
                absorb(n + 1, buf1)

        done = jnp.int32(0)
        for body_steps in PIPELINE_BODY_STEPS:
            trips = (visits - 2 - done) // body_steps

            def body(p, c, start=done, body_steps=body_steps):
                pipeline_steps(start + body_steps * p, body_steps)
                return c

            lax.fori_loop(0, trips, body, 0)
            done = done + trips * body_steps
        scores(visits - 1, buf1)
        absorb(visits - 2, buf0)
        diagonal_scores(jnp.minimum(qi + 1, seq // Q_TILE - 1), buf0)
        sub_ln_store(jnp.maximum(qi - 1, 0))
        absorb(visits - 1, buf1)

        def normalised(mp):
            a = acc_ref[mp]
            return a[:V_DIM] / a[V_DIM:V_DIM + 1]

        diff_ref[...] = normalised(0) - lam * normalised(1)
        return carry

    def sub_ln_store(qi):
        o = diff_ref[...]
        y = o * lax.rsqrt(jnp.mean(o * o, axis=0, keepdims=True) + SUBLN_EPS) * g_ref[...]
        start = qi * Q_TILE
        if not isinstance(start, int):
            start = pl.multiple_of(start, Q_TILE)
        o_ref[pl.ds(start, Q_TILE), :] = (y * (1.0 - lambda_init)).T.astype(BF16)

    diff_ref[...] = jnp.zeros(diff_ref.shape, F32)
    lax.fori_loop(0, seq // Q_TILE, query_tile, 0)
    sub_ln_store(seq // Q_TILE - 1)


def _attention(slopes, lam, qa, qb, ka, kb, vt, g_col, *, batch, seq, lambda_init):
    assert Q_TILE == K_TILE and (seq // K_TILE) % 2 == 0
    assert 8 % ATT_HEADS == 0
    t = batch * seq
    rows = lambda b, h: (b, h)
    cols = lambda b, h: (h, b)
    const = lambda b, h: (0, 0)
    return pl.pallas_call(
        functools.partial(_attn_kernel, seq=seq, lambda_init=lambda_init),
        grid=(batch, ATT_HEADS),
        in_specs=[
            pl.BlockSpec(memory_space=pltpu.SMEM),
            pl.BlockSpec((4, HEAD_DIM), const),
            pl.BlockSpec((V_DIM, seq), cols),
            pl.BlockSpec((V_DIM, seq), cols),
            pl.BlockSpec((seq, V_DIM), rows),
            pl.BlockSpec((seq, V_DIM), rows),
            pl.BlockSpec((V_AUG, seq), cols),
            pl.BlockSpec((V_DIM, 1), const),
        ],
        out_specs=pl.BlockSpec((seq, V_DIM), rows),
        out_shape=jax.ShapeDtypeStruct((t, D_ATT), BF16),
        scratch_shapes=[
            pltpu.VMEM((2, 1, Q_TILE), F32),
            pltpu.VMEM((2, V_AUG, Q_TILE), F32),
            pltpu.VMEM((2, K_TILE, Q_TILE), F32),
            pltpu.VMEM((2, K_TILE, Q_TILE), F32),
            pltpu.VMEM((2, 1, Q_TILE), F32),
            pltpu.VMEM((2, 1, Q_TILE), F32),
            pltpu.VMEM((K_TILE, Q_TILE), F32),
            pltpu.VMEM((V_DIM, Q_TILE), F32),
            pltpu.SMEM((1,), jnp.int32),
        ],
        compiler_params=pltpu.CompilerParams(
            dimension_semantics=("arbitrary", "arbitrary"), vmem_limit_bytes=VMEM_LIMIT),
        name="diff_attention",
    )(slopes, lam, qa, qb, ka, kb, vt, g_col)


def _mix_ffn_kernel(yc_ref, ycp_ref, ycn_ref, ya_ref, yap_ref, yan_ref, x_ref, xp_ref, xn_ref,
                    wo_ref, gm_ref, g_ref, wu_ref, cw_ref, cb_ref, wd_ref, gp_ref, o_ref,
                    ycat_ref, hs_ref, act_ref, *, tiles_per_seq):
    i = pl.program_id(0)
    has_prev = (i % tiles_per_seq != 0).astype(F32)
    has_next = (i % tiles_per_seq != tiles_per_seq - 1).astype(F32)

    def halo_group(before_ref, after_ref):
        row = lax.broadcasted_iota(jnp.int32, before_ref.shape, 0)
        return jnp.where(row == HALO - 1, before_ref[...], after_ref[...])

    ycat_ref[0:HALO, 0:D_CONV] = halo_group(ycp_ref, ycn_ref)
    ycat_ref[0:HALO, D_CONV:] = halo_group(yap_ref, yan_ref)
    ycat_ref[HALO:, 0:D_CONV] = yc_ref[...]
    ycat_ref[HALO:, D_CONV:] = ya_ref[...]
    mix = jnp.dot(ycat_ref[...], wo_ref[...], preferred_element_type=F32)
    gm = gm_ref[...]
    x1 = x_ref[...] + _rms_norm(mix[HALO:], gm, NORM_EPS)
    x1_halo = halo_group(xp_ref, xn_ref) + _rms_norm(mix[0:HALO], gm, NORM_EPS)
    o_ref[...] = x1

    g = g_ref[...]
    halo_row = lax.broadcasted_iota(jnp.int32, (HALO, D_MODEL), 0)
    keep = jnp.where(halo_row == HALO - 1, has_prev, jnp.where(halo_row == 0, has_next, 0.0))
    hs_ref[0:HALO, :] = (_rms_norm(x1_halo, g, NORM_EPS) * keep).astype(BF16)
    hs_ref[HALO:, :] = _rms_norm(x1, g, NORM_EPS).astype(BF16)
    hs = hs_ref[...]

    def conv_chunk(c0):
        cols = slice(c0, c0 + FFN_CHUNK)
        u = jnp.dot(hs, wu_ref[:, cols], preferred_element_type=F32)
        return _conv3_rows(u, cw_ref[:, cols], FFN_ROW_TILE) + cb_ref[:, cols]

    for c in range(D_FF // FFN_CHUNK):
        gate = conv_chunk(c * FFN_CHUNK)
        up = conv_chunk(D_FF + c * FFN_CHUNK)
        act = gate * (1.0 / (1.0 + jnp.exp(-gate))) * up
        act_ref[:, c * FFN_CHUNK:(c + 1) * FFN_CHUNK] = act.astype(BF16)

    f = jnp.dot(act_ref[...], wd_ref[...], preferred_element_type=F32)
    o_ref[...] = o_ref[...] + _rms_norm(f, gp_ref[...], NORM_EPS)


def _mix_ffn(yc, ya, x2, wo, gm, g, wu, cw, cb, wd, gp, *, seq):
    t = x2.shape[0]
    row = lambda i: (i, 0)
    const = lambda i: (0, 0)
    tile_and_halos = lambda width: [pl.BlockSpec((FFN_ROW_TILE, width), row),
                                    *_halo_specs(FFN_ROW_TILE, t, width)]
    return pl.pallas_call(
        functools.partial(_mix_ffn_kernel, tiles_per_seq=seq // FFN_ROW_TILE),
        grid=(t // FFN_ROW_TILE,),
        in_specs=[
            *tile_and_halos(D_CONV),
            *tile_and_halos(D_ATT),
            *tile_and_halos(D_MODEL),
            pl.BlockSpec((D_MODEL, D_MODEL), const, pipeline_mode=pl.Buffered(1)),
            pl.BlockSpec((1, D_MODEL), const),
            pl.BlockSpec((1, D_MODEL), const),
            pl.BlockSpec((D_MODEL, 2 * D_FF), const, pipeline_mode=pl.Buffered(1)),
            pl.BlockSpec((3, 2 * D_FF), const),
            pl.BlockSpec((1, 2 * D_FF), const),
            pl.BlockSpec((D_FF, D_MODEL), const, pipeline_mode=pl.Buffered(1)),
            pl.BlockSpec((1, D_MODEL), const),
        ],
        out_specs=pl.BlockSpec((FFN_ROW_TILE, D_MODEL), row),
        out_shape=jax.ShapeDtypeStruct((t, D_MODEL), F32),
        scratch_shapes=[
            pltpu.VMEM((HALO + FFN_ROW_TILE, D_MODEL), BF16),
            pltpu.VMEM((HALO + FFN_ROW_TILE, D_MODEL), BF16),
            pltpu.VMEM((FFN_ROW_TILE, D_FF), BF16),
        ],
        compiler_params=pltpu.CompilerParams(
            dimension_semantics=("arbitrary",), vmem_limit_bytes=MIX_FFN_VMEM_LIMIT),
        name="mix_ffn",
    )(yc, yc, yc, ya, ya, ya, x2, x2, x2, wo, gm, g, wu, cw, cb, wd, gp)


def kernel(x, g_mix_pre, w_mix_in, conv_w, g_conv_out, lambda_q1, lambda_k1, lambda_q2, lambda_k2,
           g_subln, w_mix_out, g_mix_post, g_ffn_pre, w_ffn_up, ffn_conv_w, ffn_conv_b, w_ffn_down,
           g_ffn_post):
    batch, seq, d = x.shape
    depth = g_mix_pre.shape[0]
    assert d == D_MODEL and seq % ROW_TILE == 0 and seq % Q_TILE == 0 and seq % K_TILE == 0
    assert seq % FFN_ROW_TILE == 0
    assert seq <= POS_SPLIT * POS_SPLIT
    t = batch * seq
    x2 = x.reshape(t, d)
    slopes = 2.0 ** (-8.0 * (jnp.arange(ATT_HEADS, dtype=F32) + 1.0) / ATT_HEADS)
    coef = _position_coefficients()

    for layer in range(depth):
        lambda_init = 0.8 - 0.6 * math.exp(-0.3 * layer)
        lam = jnp.stack([lambda_q1[layer], lambda_k1[layer], lambda_q2[layer], lambda_k2[layer]]).astype(F32)
        yc, ka, kb, qa, qb, vt = _in_proj(x2, g_mix_pre[layer][None], coef, conv_w[layer],
                                          g_conv_out[layer][None], w_mix_in[layer].astype(BF16),
                                          seq=seq)
        ya = _attention(slopes, lam, qa, qb, ka, kb, vt, g_subln[layer][:, None],
                        batch=batch, seq=seq, lambda_init=lambda_init)
        x2 = _mix_ffn(yc, ya, x2, w_mix_out[layer].astype(BF16), g_mix_post[layer][None],
                      g_ffn_pre[layer][None], w_ffn_up[layer].astype(BF16), ffn_conv_w[layer],
                      ffn_conv_b[layer][None], w_ffn_down[layer].astype(BF16),
                      g_ffn_post[layer][None], seq=seq)
    return x2.reshape(batch, seq, d)
```
